```python
import jax, jax.numpy as jnp
from jax import lax
import numpy as np

D_MODEL = 2048
BATCH = 4
SEQ = 4096
DEPTH = 2

GRID_W = 64
HEAD_DIM = 128
A_HEADS = 4
WIN_H_MAX = 8
WIN_W = 16
B_HEADS = 8
B_KV_HEADS = 2
C_HEADS = 4
C_Q_RANK = 512
C_KV_RANK = 256
C_NOPE = 128
C_ROPE = 64
C_V = 128
A_WIDTH = A_HEADS * HEAD_DIM
B_WIDTH = B_HEADS * HEAD_DIM
B_KV_WIDTH = B_KV_HEADS * HEAD_DIM
C_WIDTH = C_HEADS * C_V
D_MIX = A_WIDTH + B_WIDTH + C_WIDTH
D_IN = 3 * A_WIDTH + B_WIDTH + 2 * B_KV_WIDTH + C_Q_RANK + C_KV_RANK + C_ROPE
D_FF_DENSE = 5632
N_EXPERTS = 8
TOP_K = 2
D_FF_EXPERT = 7168
N_DENSE = (DEPTH + 1) // 2
N_MOE = DEPTH // 2
Q_BLOCK = 128
ROPE_THETA = 10000.0
EPS = 1e-6

kernel_name = 'hymba_style_natten_gqa_mla_moe_encoder'


def rms_norm(x, g):
    xf = x.astype(jnp.float32)
    y = xf * lax.rsqrt(jnp.mean(xf * xf, axis=-1, keepdims=True) + EPS)
    return (y * g.astype(jnp.float32)).astype(x.dtype)


def rope_1d(x, pos):
    half = x.shape[-1] // 2
    freqs = ROPE_THETA ** (-jnp.arange(half, dtype=jnp.float32) / half)
    ang = pos.astype(jnp.float32)[:, None] * freqs[None, :]
    cos = jnp.cos(ang)[None, :, None, :]
    sin = jnp.sin(ang)[None, :, None, :]
    xf = x.astype(jnp.float32)
    x1, x2 = xf[..., :half], xf[..., half:]
    return jnp.concatenate([x1 * cos - x2 * sin, x2 * cos + x1 * sin], axis=-1).astype(x.dtype)


def rope_2d(x, rows, cols):
    d = x.shape[-1]
    return jnp.concatenate([rope_1d(x[..., : d // 2], rows), rope_1d(x[..., d // 2 :], cols)], axis=-1)


def blocked_attention(q, k, v):
    B, S, Hq, dq = q.shape
    Hkv, dv = k.shape[2], v.shape[-1]
    G = Hq // Hkv
    scale = dq ** -0.5
    qb = q.reshape(B, S // Q_BLOCK, Q_BLOCK, Hkv, G, dq).transpose(1, 0, 2, 3, 4, 5)

    def one_block(q_blk):
        s = jnp.einsum('bqhgd,bkhd->bhgqk', q_blk, k, preferred_element_type=jnp.float32) * scale
        p = jax.nn.softmax(s, axis=-1)
        return jnp.einsum('bhgqk,bkhd->bqhgd', p.astype(v.dtype), v)

    out = lax.map(one_block, qb)
    return out.transpose(1, 0, 2, 3, 4, 5).reshape(B, S, Hq, dv)


def neighbourhood_attention(q, k, v, rpb):
    B, S, H, d = q.shape
    rows = S // GRID_W
    kh = min(WIN_H_MAX, rows)
    scale = d ** -0.5
    qg = q.reshape(B, rows, GRID_W, H, d)
    kg = k.reshape(B, rows, GRID_W, H, d)
    vg = v.reshape(B, rows, GRID_W, H, d)
    cols = jnp.arange(GRID_W)
    c0 = jnp.clip(cols - WIN_W // 2, 0, GRID_W - WIN_W)
    col_idx = c0[:, None] + jnp.arange(WIN_W)[None, :]
    col_bias_idx = col_idx - cols[:, None] + (WIN_W - 1)

    def one_row(args):
        r, q_row = args
        r0 = jnp.clip(r - kh // 2, 0, rows - kh)
        k_rows = lax.dynamic_slice_in_dim(kg, r0, kh, axis=1)
        v_rows = lax.dynamic_slice_in_dim(vg, r0, kh, axis=1)
        k_win = k_rows[:, :, col_idx]
        v_win = v_rows[:, :, col_idx]
        s = jnp.einsum('bchd,bicjhd->bhcij', q_row, k_win, preferred_element_type=jnp.float32) * scale
        row_bias_idx = r0 + jnp.arange(kh) - r + (WIN_H_MAX - 1)
        bias = rpb[:, row_bias_idx[:, None, None], col_bias_idx[None, :, :]]
        s = s + bias.transpose(0, 2, 1, 3)[None].astype(jnp.float32)
        p = jax.nn.softmax(s, axis=(-2, -1))
        return jnp.einsum('bhcij,bicjhd->bchd', p.astype(v.dtype), v_win)

    out = lax.map(one_row, (jnp.arange(rows), qg.transpose(1, 0, 2, 3, 4)))
    return out.transpose(1, 0, 2, 3, 4).reshape(B, S, H, d)


def split_columns(proj):
    sizes = (A_WIDTH, A_WIDTH, A_WIDTH, B_WIDTH, B_KV_WIDTH, B_KV_WIDTH, C_Q_RANK, C_KV_RANK, C_ROPE)
    idx, acc = [], 0
    for s in sizes[:-1]:
        acc += s
        idx.append(acc)
    return jnp.split(proj, idx, axis=-1)


def hybrid_mixer(h, rows, cols, w_in, rpb, qn_a, kn_a, qn_b, kn_b, cq_norm, ckv_norm,
                 w_uq, w_ukv, qn_c, kn_c, on_a, on_b, on_c, w_out):
    B, S, _ = h.shape
    proj = h @ w_in
    q_a, k_a, v_a, q_b, k_b, v_b, cq, ckv, kpe = split_columns(proj)

    q_a = rms_norm(q_a.reshape(B, S, A_HEADS, HEAD_DIM), qn_a)
    k_a = rms_norm(k_a.reshape(B, S, A_HEADS, HEAD_DIM), kn_a)
    v_a = v_a.reshape(B, S, A_HEADS, HEAD_DIM)
    o_a = neighbourhood_attention(q_a, k_a, v_a, rpb).reshape(B, S, A_WIDTH)

    q_b = rope_2d(rms_norm(q_b.reshape(B, S, B_HEADS, HEAD_DIM), qn_b), rows, cols)
    k_b = rope_2d(rms_norm(k_b.reshape(B, S, B_KV_HEADS, HEAD_DIM), kn_b), rows, cols)
    v_b = v_b.reshape(B, S, B_KV_HEADS, HEAD_DIM)
    o_b = blocked_attention(q_b, k_b, v_b).reshape(B, S, B_WIDTH)

    cq = rms_norm(cq, cq_norm)
    q_c = (cq @ w_uq).reshape(B, S, C_HEADS, C_NOPE + C_ROPE)
    ckv = rms_norm(ckv, ckv_norm)
    kv = (ckv @ w_ukv).reshape(B, S, C_HEADS, C_NOPE + C_V)
    k_nope, v_c = kv[..., :C_NOPE], kv[..., C_NOPE:]
    q_nope = rms_norm(q_c[..., :C_NOPE], qn_c[:C_NOPE])
    q_pe = rope_2d(rms_norm(q_c[..., C_NOPE:], qn_c[C_NOPE:]), rows, cols)
    k_nope = rms_norm(k_nope, kn_c[:C_NOPE])
    k_pe = rope_2d(rms_norm(kpe.reshape(B, S, 1, C_ROPE), kn_c[C_NOPE:]), rows, cols)
    q_full = jnp.concatenate([q_nope, q_pe], axis=-1)
    k_full = jnp.concatenate([k_nope, jnp.broadcast_to(k_pe, (B, S, C_HEADS, C_ROPE))], axis=-1)
    o_c = blocked_attention(q_full, k_full, v_c).reshape(B, S, C_WIDTH)

    o = jnp.concatenate([rms_norm(o_a, on_a), rms_norm(o_b, on_b), rms_norm(o_c, on_c)], axis=-1)
    return o @ w_out


def swiglu(t, w_gate, w_up, w_down):
    return (jax.nn.silu(t @ w_gate) * (t @ w_up)) @ w_down


def moe_swiglu(h, router, we_gate, we_up, we_down):
    B, S, D = h.shape
    t = h.reshape(B * S, D)
    logits = (t @ router).astype(jnp.float32)
    top_vals, top_idx = lax.top_k(logits, TOP_K)
    top_w = jax.nn.softmax(top_vals, axis=-1)
    gates = jnp.sum(jax.nn.one_hot(top_idx, N_EXPERTS, dtype=jnp.float32) * top_w[..., None], axis=1)
    out = jnp.zeros_like(t)
    for e in range(N_EXPERTS):
        out = out + gates[:, e:e + 1].astype(t.dtype) * swiglu(t, we_gate[e], we_up[e], we_down[e])
    return out.reshape(B, S, D)


def setup_inputs(seed: int = 0) -> dict:
    key = jax.random.key(seed)
    ks = jax.random.split(key, 26)

    def nrm(k, shape, scale):
        return jax.random.normal(k, shape, dtype=jnp.float32) * scale

    def gain(k, shape):
        return 1.0 + 0.02 * jax.random.normal(k, shape, dtype=jnp.float32)

    return {
        'x': nrm(ks[0], (BATCH, SEQ, D_MODEL), 1.0),
        'mix_norm': gain(ks[1], (DEPTH, D_MODEL)),
        'w_in': nrm(ks[2], (DEPTH, D_MODEL, D_IN), D_MODEL ** -0.5),
        'rpb': nrm(ks[3], (DEPTH, A_HEADS, 2 * WIN_H_MAX - 1, 2 * WIN_W - 1), 0.1),
        'qn_a': gain(ks[4], (DEPTH, HEAD_DIM)),
        'kn_a': gain(ks[5], (DEPTH, HEAD_DIM)),
        'qn_b': gain(ks[6], (DEPTH, HEAD_DIM)),
        'kn_b': gain(ks[7], (DEPTH, HEAD_DIM)),
        'cq_norm': gain(ks[8], (DEPTH, C_Q_RANK)),
        'ckv_norm': gain(ks[9], (DEPTH, C_KV_RANK)),
        'w_uq': nrm(ks[10], (DEPTH, C_Q_RANK, C_HEADS * (C_NOPE + C_ROPE)), C_Q_RANK ** -0.5),
        'w_ukv': nrm(ks[11], (DEPTH, C_KV_RANK, C_HEADS * (C_NOPE + C_V)), C_KV_RANK ** -0.5),
        'qn_c': gain(ks[12], (DEPTH, C_NOPE + C_ROPE)),
        'kn_c': gain(ks[13], (DEPTH, C_NOPE + C_ROPE)),
        'on_a': gain(ks[14], (DEPTH, A_WIDTH)),
        'on_b': gain(ks[15], (DEPTH, B_WIDTH)),
        'on_c': gain(ks[16], (DEPTH, C_WIDTH)),
        'w_out': nrm(ks[17], (DEPTH, D_MIX, D_MODEL), D_MIX ** -0.5),
        'ffn_norm': gain(ks[18], (DEPTH, D_MODEL)),
        'w_gate': nrm(ks[19], (N_DENSE, D_MODEL, D_FF_DENSE), D_MODEL ** -0.5),
        'w_up': nrm(ks[20], (N_DENSE, D_MODEL, D_FF_DENSE), D_MODEL ** -0.5),
        'w_down': nrm(ks[21], (N_DENSE, D_FF_DENSE, D_MODEL), D_FF_DENSE ** -0.5),
        'router': nrm(ks[22], (N_MOE, D_MODEL, N_EXPERTS), D_MODEL ** -0.5),
        'we_gate': nrm(ks[23], (N_MOE, N_EXPERTS, D_MODEL, D_FF_EXPERT), D_MODEL ** -0.5),
        'we_up': nrm(ks[24], (N_MOE, N_EXPERTS, D_MODEL, D_FF_EXPERT), D_MODEL ** -0.5),
        'we_down': nrm(ks[25], (N_MOE, N_EXPERTS, D_FF_EXPERT, D_MODEL), D_FF_EXPERT ** -0.5),
    }


def reference(x, mix_norm, w_in, rpb, qn_a, kn_a, qn_b, kn_b, cq_norm, ckv_norm, w_uq, w_ukv,
              qn_c, kn_c, on_a, on_b, on_c, w_out, ffn_norm, w_gate, w_up, w_down,
              router, we_gate, we_up, we_down):
    S = x.shape[1]
    pos = jnp.arange(S, dtype=jnp.int32)
    rows = pos // GRID_W
    cols = pos % GRID_W
    for l in range(DEPTH):
        h = rms_norm(x, mix_norm[l])
        x = x + hybrid_mixer(h, rows, cols, w_in[l], rpb[l], qn_a[l], kn_a[l], qn_b[l], kn_b[l],
                             cq_norm[l], ckv_norm[l], w_uq[l], w_ukv[l], qn_c[l], kn_c[l],
                             on_a[l], on_b[l], on_c[l], w_out[l])
        h = rms_norm(x, ffn_norm[l])
        if l % 2 == 0:
            x = x + swiglu(h, w_gate[l // 2], w_up[l // 2], w_down[l // 2])
        else:
            x = x + moe_swiglu(h, router[l // 2], we_gate[l // 2], we_up[l // 2], we_down[l // 2])
    return x
```

```python
import functools
import math

import jax
import jax.numpy as jnp
from jax import lax
from jax.experimental import pallas as pl
from jax.experimental.pallas import tpu as pltpu

D_MODEL = 2048
SEQ = 4096
GRID_W = 64
GRID_ROWS = SEQ // GRID_W
HEAD_DIM = 128
A_HEADS = 4
WIN_H = 8
WIN_W = 16
B_HEADS = 8
B_KV_HEADS = 2
C_HEADS = 4
C_Q_RANK = 512
C_KV_RANK = 256
C_NOPE = 128
C_ROPE = 64
C_V = 128
A_WIDTH = A_HEADS * HEAD_DIM
B_WIDTH = B_HEADS * HEAD_DIM
B_KV_WIDTH = B_KV_HEADS * HEAD_DIM
C_WIDTH = C_HEADS * C_V
C_QK_PAD = 256
D_IN = 3 * A_WIDTH + B_WIDTH + 2 * B_KV_WIDTH + C_Q_RANK + C_KV_RANK + C_ROPE
D_IN_PAD = D_IN + 64
N_EXPERTS = 8
ROPE_THETA = 10000.0
EPS = 1e-6
NEG_BIG = -1e30

LANES = 128
MIB = 1024 * 1024

BF16 = jnp.bfloat16
F32 = jnp.float32

TM_IN = 512
TQ_NBR = 4 * GRID_W
TQ_FLASH = 256
TK_FLASH = 512
TM_OUT = 512
TM_FFN = 1024
TF_FFN = 256
TB_ROUTE = 512
TB_DISPATCH = 256
TB_COMBINE = 256

_TRANS_B = (((1,), (1,)), ((), ()))


def _params(semantics, vmem_mib):
    return pltpu.CompilerParams(dimension_semantics=semantics,
                                vmem_limit_bytes=vmem_mib * MIB)


def _rms(y, gain, n):
    ms = jnp.sum(y * y, axis=-1, keepdims=True) * (1.0 / n)
    return y * lax.rsqrt(ms + EPS) * gain


def _rope(y, cos_t, sin_t, half):
    lane = lax.broadcasted_iota(jnp.int32, y.shape, 1)
    first = (lane & half) == 0
    swapped = jnp.where(first, pltpu.roll(y, LANES - half, 1), pltpu.roll(y, half, 1))
    return y * cos_t + swapped * sin_t


def _in_proj_kernel(x_ref, gmix_ref, w_ref, wuq_ref, wukv_ref, gains_ref,
                    cosb_ref, sinb_ref, cosc_ref, sinc_ref,
                    qa_ref, ka_ref, va_ref, qb_ref, kb_ref, vb_ref,
                    qc_ref, kc_ref, vc_ref):
    x = x_ref[...]
    xn = _rms(x, gmix_ref[...], D_MODEL).astype(BF16)

    def proj(c0, c1):
        return jnp.dot(xn, w_ref[:, c0:c1], preferred_element_type=F32)

    def gain(row, width=LANES):
        return gains_ref[row:row + 1, 0:width]

    cosb, sinb = cosb_ref[...], sinb_ref[...]
    cosc, sinc = cosc_ref[...], sinc_ref[...]

    col = 0
    y = proj(col, col + A_WIDTH)
    for h in range(A_HEADS):
        sl = slice(h * HEAD_DIM, (h + 1) * HEAD_DIM)
        qa_ref[:, sl] = _rms(y[:, sl], gain(0), HEAD_DIM).astype(BF16)
    col += A_WIDTH
    y = proj(col, col + A_WIDTH)
    for h in range(A_HEADS):
        sl = slice(h * HEAD_DIM, (h + 1) * HEAD_DIM)
        ka_ref[:, sl] = _rms(y[:, sl], gain(1), HEAD_DIM).astype(BF16)
    col += A_WIDTH
    va_ref[...] = proj(col, col + A_WIDTH).astype(BF16)
    col += A_WIDTH

    for c in range(B_WIDTH // 512):
        y = proj(col + 512 * c, col + 512 * (c + 1))
        for hh in range(4):
            sl = slice(hh * HEAD_DIM, (hh + 1) * HEAD_DIM)
            out = _rope(_rms(y[:, sl], gain(2), HEAD_DIM), cosb, sinb, 32)
            h = 4 * c + hh
            qb_ref[:, h * HEAD_DIM:(h + 1) * HEAD_DIM] = out.astype(BF16)
    col += B_WIDTH
    y = proj(col, col + 2 * B_KV_WIDTH)
    for h in range(B_KV_HEADS):
        sl = slice(h * HEAD_DIM, (h + 1) * HEAD_DIM)
        kb_ref[:, sl] = _rope(_rms(y[:, sl], gain(3), HEAD_DIM), cosb, sinb, 32).astype(BF16)
    vb_ref[...] = y[:, B_KV_WIDTH:].astype(BF16)
    col += 2 * B_KV_WIDTH

    lane = lax.broadcasted_iota(jnp.int32, (x.shape[0], LANES), 1)
    low = lane < C_ROPE
    cqn = _rms(proj(col, col + C_Q_RANK), gain(4, C_Q_RANK), C_Q_RANK).astype(BF16)
    col += C_Q_RANK
    yq = jnp.dot(cqn, wuq_ref[...], preferred_element_type=F32)
    for p in range(C_HEADS // 2):
        pe2 = yq[:, C_HEADS * C_NOPE + p * LANES:C_HEADS * C_NOPE + (p + 1) * LANES]
        sq = pe2 * pe2
        ms_lo = jnp.sum(jnp.where(low, sq, 0.0), axis=-1, keepdims=True) * (1.0 / C_ROPE)
        ms_hi = jnp.sum(jnp.where(low, 0.0, sq), axis=-1, keepdims=True) * (1.0 / C_ROPE)
        r = jnp.where(low, lax.rsqrt(ms_lo + EPS), lax.rsqrt(ms_hi + EPS))
        roped = _rope(pe2 * r * gain(7), cosc, sinc, 16)
        for hh in range(2):
            h = 2 * p + hh
            nope = _rms(yq[:, h * C_NOPE:(h + 1) * C_NOPE], gain(6), C_NOPE)
            pe = roped if hh == 0 else pltpu.roll(roped, C_ROPE, 1)
            base = h * C_QK_PAD
            qc_ref[:, base:base + C_NOPE] = nope.astype(BF16)
            qc_ref[:, base + C_NOPE:base + C_QK_PAD] = jnp.where(low, pe, 0.0).astype(BF16)

    ckvn = _rms(proj(col, col + C_KV_RANK), gain(5, C_KV_RANK), C_KV_RANK).astype(BF16)
    col += C_KV_RANK
    ykv = jnp.dot(ckvn, wukv_ref[...], preferred_element_type=F32)
    ype = proj(col, col + LANES)
    ms = jnp.sum(ype * ype, axis=-1, keepdims=True) * (1.0 / C_ROPE)
    kpe = _rope(ype * lax.rsqrt(ms + EPS) * gain(9), cosc, sinc, 16)
    kpe = jnp.where(low, kpe, 0.0).astype(BF16)
    for h in range(C_HEADS):
        base = h * C_QK_PAD
        k_nope = _rms(ykv[:, h * C_NOPE:(h + 1) * C_NOPE], gain(8), C_NOPE)
        kc_ref[:, base:base + C_NOPE] = k_nope.astype(BF16)
        kc_ref[:, base + C_NOPE:base + C_QK_PAD] = kpe
    vc_ref[...] = ykv[:, C_HEADS * C_NOPE:].astype(BF16)


def _in_proj(x2, gmix, w_pad, wuq, wukv, gains, cosb, sinb, cosc, sinc):
    n = x2.shape[0]
    tm = TM_IN
    seq_blocks = SEQ // tm
    row = lambda i: (i, 0)
    const = lambda i: (0, 0)
    pos = lambda i: (i % seq_blocks, 0)
    resident = pl.Buffered(1)
    widths = (A_WIDTH, A_WIDTH, A_WIDTH, B_WIDTH, B_KV_WIDTH, B_KV_WIDTH,
              C_HEADS * C_QK_PAD, C_HEADS * C_QK_PAD, C_WIDTH)
    return pl.pallas_call(
        _in_proj_kernel,
        grid=(n // tm,),
        in_specs=[
            pl.BlockSpec((tm, D_MODEL), row),
            pl.BlockSpec((1, D_MODEL), const),
            pl.BlockSpec((D_MODEL, D_IN_PAD), const, pipeline_mode=resident),
            pl.BlockSpec(wuq.shape, const, pipeline_mode=resident),
            pl.BlockSpec(wukv.shape, const, pipeline_mode=resident),
            pl.BlockSpec(gains.shape, const),
            pl.BlockSpec((tm, LANES), pos), pl.BlockSpec((tm, LANES), pos),
            pl.BlockSpec((tm, LANES), pos), pl.BlockSpec((tm, LANES), pos),
        ],
        out_specs=[pl.BlockSpec((tm, w), row) for w in widths],
        out_shape=[jax.ShapeDtypeStruct((n, w), BF16) for w in widths],
        compiler_params=_params(("parallel",), 56),
        name="in_proj",
    )(x2, gmix, w_pad, wuq, wukv, gains, cosb, sinb, cosc, sinc)


def _nbr_kernel(q_ref, k0_ref, k1_ref, k2_ref, v0_ref, v1_ref, v2_ref,
                bias_ref, gain_ref, o_ref):
    scale = HEAD_DIM ** -0.5
    k_refs = (k0_ref, k1_ref, k2_ref)
    v_refs = (v0_ref, v1_ref, v2_ref)
    outs = []
    for h in range(A_HEADS):
        sl = slice(h * HEAD_DIM, (h + 1) * HEAD_DIM)
        q = q_ref[:, sl]
        s = []
        for j in range(3):
            sj = lax.dot_general(q, k_refs[j][:, sl], _TRANS_B, preferred_element_type=F32)
            s.append(sj * scale + bias_ref[h, :, j * TQ_NBR:(j + 1) * TQ_NBR])
        m = jnp.maximum(jnp.maximum(jnp.max(s[0], axis=-1, keepdims=True),
                                    jnp.max(s[1], axis=-1, keepdims=True)),
                        jnp.max(s[2], axis=-1, keepdims=True))
        p = [jnp.exp(sj - m) for sj in s]
        l = (jnp.sum(p[0], axis=-1, keepdims=True) + jnp.sum(p[1], axis=-1, keepdims=True)
             + jnp.sum(p[2], axis=-1, keepdims=True))
        o = jnp.dot(p[0].astype(BF16), v_refs[0][:, sl], preferred_element_type=F32)
        o += jnp.dot(p[1].astype(BF16), v_refs[1][:, sl], preferred_element_type=F32)
        o += jnp.dot(p[2].astype(BF16), v_refs[2][:, sl], preferred_element_type=F32)
        outs.append(o / l)
    o_all = jnp.concatenate(outs, axis=-1)
    o_ref[...] = _rms(o_all, gain_ref[...], A_WIDTH).astype(BF16)


def _nbr_bias_tables(rpb_l):
    u_rep = jnp.array([0, 1, GRID_ROWS // 4 - 1], jnp.int32)
    start = 4 * jnp.clip(u_rep - 1, 0, GRID_ROWS // 4 - 3)
    qi = jnp.arange(TQ_NBR)
    ki = jnp.arange(3 * TQ_NBR)
    qr = 4 * u_rep[:, None] + (qi // GRID_W)[None, :]
    qc = (qi % GRID_W)[None, :]
    kr = start[:, None] + (ki // GRID_W)[None, :]
    kc = (ki % GRID_W)[None, :]
    r0 = jnp.clip(qr - WIN_H // 2, 0, GRID_ROWS - WIN_H)
    c0 = jnp.clip(qc - WIN_W // 2, 0, GRID_W - WIN_W)
    dr = kr[:, None, :] - qr[:, :, None]
    dc = kc[:, None, :] - qc[:, :, None]
    ok_r = (kr[:, None, :] >= r0[:, :, None]) & (kr[:, None, :] < r0[:, :, None] + WIN_H)
    ok_c = (kc[:, None, :] >= c0[:, :, None]) & (kc[:, None, :] < c0[:, :, None] + WIN_W)
    ri = jnp.clip(dr + WIN_H - 1, 0, 2 * WIN_H - 2)
    ci = jnp.clip(dc + WIN_W - 1, 0, 2 * WIN_W - 2)
    vals = rpb_l[:, ri, ci]
    return jnp.where((ok_r & ok_c)[None], vals.astype(F32), NEG_BIG)


def _nbr_attention(qa, ka, va, bias, gain):
    n = qa.shape[0]
    batch = n // SEQ
    blocks = SEQ // TQ_NBR

    def q_map(b, u):
        return (b * blocks + u, 0)

    def kv_map(j):
        return lambda b, u: (b * blocks + jnp.clip(u - 1, 0, blocks - 3) + j, 0)

    def bias_map(b, u):
        cls = jnp.where(u == 0, 0, jnp.where(u == blocks - 1, 2, 1))
        return (0, cls, 0, 0)

    tile = lambda m: pl.BlockSpec((TQ_NBR, A_WIDTH), m)
    return pl.pallas_call(
        _nbr_kernel,
        grid=(batch, blocks),
        in_specs=[tile(q_map), tile(kv_map(0)), tile(kv_map(1)), tile(kv_map(2)),
                  tile(kv_map(0)), tile(kv_map(1)), tile(kv_map(2)),
                  pl.BlockSpec((A_HEADS, None, TQ_NBR, 3 * TQ_NBR), bias_map),
                  pl.BlockSpec((1, A_WIDTH), lambda b, u: (0, 0))],
        out_specs=tile(q_map),
        out_shape=jax.ShapeDtypeStruct((n, A_WIDTH), BF16),
        compiler_params=_params(("parallel", "arbitrary"), 40),
        name="nbr_attention",
    )(qa, ka, ka, ka, va, va, va, bias, gain)


def _flash_kernel(q_ref, k_ref, v_ref, gain_ref, o_ref, acc_ref, *,
                  n_heads, group, dq, dv, tk, exp2_scale):
    tq = q_ref.shape[0]
    n_kv = k_ref.shape[0] // tk
    for h in range(n_heads):
        g = h // group
        q = q_ref[:, h * dq:(h + 1) * dq]

        def body(j, carry, q=q, g=g):
            m, l, acc = carry
            off = pl.multiple_of(j * tk, tk)
            k = k_ref[pl.ds(off, tk), g * dq:(g + 1) * dq]
            v = v_ref[pl.ds(off, tk), g * dv:(g + 1) * dv]
            s = lax.dot_general(q, k, _TRANS_B, preferred_element_type=F32)
            m_new = jnp.maximum(m, jnp.max(s, axis=-1, keepdims=True))
            alpha = jnp.exp2((m - m_new) * exp2_scale)
            p = jnp.exp2((s - m_new) * exp2_scale)
            l = alpha * l + jnp.sum(p, axis=-1, keepdims=True)
            acc = alpha * acc + jnp.dot(p.astype(BF16), v, preferred_element_type=F32)
            return m_new, l, acc

        init = (jnp.full((tq, 1), -jnp.inf, F32), jnp.zeros((tq, 1), F32),
                jnp.zeros((tq, dv), F32))
        _, l, acc = lax.fori_loop(0, n_kv, body, init)
        acc_ref[:, h * dv:(h + 1) * dv] = acc / l
    width = n_heads * dv
    o_ref[...] = _rms(acc_ref[...], gain_ref[...], width).astype(BF16)


def _flash_attention(q, k, v, gain, *, n_heads, n_kv_heads, dq, dv, scale):
    n = q.shape[0]
    batch = n // SEQ
    tq = TQ_FLASH
    q_blocks = SEQ // tq
    width = n_heads * dv
    kernel = functools.partial(
        _flash_kernel, n_heads=n_heads, group=n_heads // n_kv_heads, dq=dq, dv=dv,
        tk=TK_FLASH, exp2_scale=scale * math.log2(math.e))
    return pl.pallas_call(
        kernel,
        grid=(batch, q_blocks),
        in_specs=[pl.BlockSpec((tq, n_heads * dq), lambda b, i: (b * q_blocks + i, 0)),
                  pl.BlockSpec((SEQ, n_kv_heads * dq), lambda b, i: (b, 0)),
                  pl.BlockSpec((SEQ, n_kv_heads * dv), lambda b, i: (b, 0)),
                  pl.BlockSpec((1, width), lambda b, i: (0, 0))],
        out_specs=pl.BlockSpec((tq, width), lambda b, i: (b * q_blocks + i, 0)),
        out_shape=jax.ShapeDtypeStruct((n, width), BF16),
        scratch_shapes=[pltpu.VMEM((tq, width), F32)],
        compiler_params=_params(("parallel", "arbitrary"), 48),
        name=f"flash_attention_h{n_heads}",
    )(q, k, v, gain)


def _out_proj_kernel(x_ref, oa_ref, ob_ref, oc_ref, wa_ref, wb_ref, wc_ref, g_ref,
                     x1_ref, *h_ref):
    acc = x_ref[...] + jnp.dot(oa_ref[...], wa_ref[...], preferred_element_type=F32)
    acc += jnp.dot(ob_ref[...], wb_ref[...], preferred_element_type=F32)
    acc += jnp.dot(oc_ref[...], wc_ref[...], preferred_element_type=F32)
    x1_ref[...] = acc
    if h_ref:
        h_ref[0][...] = _rms(acc, g_ref[...], D_MODEL).astype(BF16)


def _out_proj(x2, oa, ob, oc, wa, wb, wc, g_ffn, emit_h):
    n = x2.shape[0]
    tm = TM_OUT
    row = lambda i: (i, 0)
    const = lambda i: (0, 0)
    resident = pl.Buffered(1)
    out_specs = [pl.BlockSpec((tm, D_MODEL), row)]
    out_shape = [jax.ShapeDtypeStruct((n, D_MODEL), F32)]
    if emit_h:
        out_specs.append(pl.BlockSpec((tm, D_MODEL), row))
        out_shape.append(jax.ShapeDtypeStruct((n, D_MODEL), BF16))
    return pl.pallas_call(
        _out_proj_kernel,
        grid=(n // tm,),
        in_specs=[pl.BlockSpec((tm, D_MODEL), row),
                  pl.BlockSpec((tm, A_WIDTH), row),
                  pl.BlockSpec((tm, B_WIDTH), row),
                  pl.BlockSpec((tm, C_WIDTH), row),
                  pl.BlockSpec(wa.shape, const, pipeline_mode=resident),
                  pl.BlockSpec(wb.shape, const, pipeline_mode=resident),
                  pl.BlockSpec(wc.shape, const, pipeline_mode=resident),
                  pl.BlockSpec((1, D_MODEL), const)],
        out_specs=out_specs,
        out_shape=out_shape,
        compiler_params=_params(("parallel",), 48),
        name="out_proj",
    )(x2, oa, ob, oc, wa, wb, wc, g_ffn)


def _swiglu_step(h, wg_ref, wu_ref, wd_ref):
    gate = jnp.dot(h, wg_ref[...].astype(BF16), preferred_element_type=F32)
    up = jnp.dot(h, wu_ref[...].astype(BF16), preferred_element_type=F32)
    act = (gate * jax.nn.sigmoid(gate) * up).astype(BF16)
    return jnp.dot(act, wd_ref[...].astype(BF16), preferred_element_type=F32)


def _dense_ffn_kernel(x_ref, g_ref, wg_ref, wu_ref, wd_ref, o_ref, h_ref):
    @pl.when(pl.program_id(1) == 0)
    def _():
        x = x_ref[...]
        h_ref[...] = _rms(x, g_ref[...], D_MODEL).astype(BF16)
        o_ref[...] = x

    o_ref[...] += _swiglu_step(h_ref[...], wg_ref, wu_ref, wd_ref)


def _dense_ffn(x2, g, w_gate, w_up, w_down):
    n = x2.shape[0]
    d_ff = w_gate.shape[1]
    tm, tf = TM_FFN, TF_FFN
    return pl.pallas_call(
        _dense_ffn_kernel,
        grid=(n // tm, d_ff // tf),
        in_specs=[pl.BlockSpec((tm, D_MODEL), lambda i, f: (i, 0), pipeline_mode=pl.Buffered(1)),
                  pl.BlockSpec((1, D_MODEL), lambda i, f: (0, 0)),
                  pl.BlockSpec((D_MODEL, tf), lambda i, f: (0, f)),
                  pl.BlockSpec((D_MODEL, tf), lambda i, f: (0, f)),
                  pl.BlockSpec((tf, D_MODEL), lambda i, f: (f, 0))],
        out_specs=pl.BlockSpec((tm, D_MODEL), lambda i, f: (i, 0)),
        out_shape=jax.ShapeDtypeStruct((n, D_MODEL), F32),
        scratch_shapes=[pltpu.VMEM((tm, D_MODEL), BF16)],
        compiler_params=_params(("parallel", "arbitrary"), 56),
        name="dense_ffn",
    )(x2, g, w_gate, w_up, w_down)


def _moe_ffn_kernel(tile_expert_ref, n_tiles_ref, h_ref, wg_ref, wu_ref, wd_ref, o_ref):
    i, f = pl.program_id(0), pl.program_id(1)

    @pl.when(i < n_tiles_ref[0])
    def _():
        y = _swiglu_step(h_ref[...], wg_ref, wu_ref, wd_ref)

        @pl.when(f == 0)
        def _():
            o_ref[...] = y

        @pl.when(f > 0)
        def _():
            o_ref[...] += y

    @pl.when(jnp.logical_and(i >= n_tiles_ref[0], f == 0))
    def _():
        o_ref[...] = jnp.zeros_like(o_ref)


def _moe_ffn(tile_expert, n_tiles, hs, we_gate, we_up, we_down):
    p_rows = hs.shape[0]
    d_ff = we_gate.shape[2]
    tm, tf = TM_FFN, TF_FFN
    n_f = d_ff // tf

    def row_map(i, f, te, nt):
        return (jnp.minimum(i, nt[0] - 1), 0)

    def w_col_map(i, f, te, nt):
        valid = i < nt[0]
        return (te[jnp.minimum(i, nt[0] - 1)], 0, jnp.where(valid, f, n_f - 1))

    def w_row_map(i, f, te, nt):
        valid = i < nt[0]
        return (te[jnp.minimum(i, nt[0] - 1)], jnp.where(valid, f, n_f - 1), 0)

    grid_spec = pltpu.PrefetchScalarGridSpec(
        num_scalar_prefetch=2,
        grid=(p_rows // tm, n_f),
        in_specs=[pl.BlockSpec((tm, D_MODEL), row_map),
                  pl.BlockSpec((None, D_MODEL, tf), w_col_map),
                  pl.BlockSpec((None, D_MODEL, tf), w_col_map),
                  pl.BlockSpec((None, tf, D_MODEL), w_row_map)],
        out_specs=pl.BlockSpec((tm, D_MODEL), lambda i, f, te, nt: (i, 0)),
    )
    return pl.pallas_call(
        _moe_ffn_kernel,
        grid_spec=grid_spec,
        out_shape=jax.ShapeDtypeStruct((p_rows, D_MODEL), F32),
        compiler_params=_params(("arbitrary", "arbitrary"), 56),
        name="moe_ffn",
    )(tile_expert, n_tiles, hs, we_gate, we_up, we_down)


def _router_kernel(h_ref, w_ref, meta_ref, count_ref, carry_ref):
    @pl.when(pl.program_id(0) == 0)
    def _():
        carry_ref[...] = jnp.zeros_like(carry_ref)

    tb = h_ref.shape[0]
    logits = jnp.dot(h_ref[...], w_ref[...], preferred_element_type=F32)
    lane = lax.broadcasted_iota(jnp.int32, logits.shape, 1)
    lg = jnp.where(lane < N_EXPERTS, logits, -jnp.inf)
    m1 = jnp.max(lg, axis=-1, keepdims=True)
    i1 = jnp.min(jnp.where(lg == m1, lane, LANES), axis=-1, keepdims=True)
    lg2 = jnp.where(lane == i1, -jnp.inf, lg)
    m2 = jnp.max(lg2, axis=-1, keepdims=True)
    i2 = jnp.min(jnp.where(lg2 == m2, lane, LANES), axis=-1, keepdims=True)
    e2 = jnp.exp(m2 - m1)
    w1 = 1.0 / (1.0 + e2)
    w2 = e2 / (1.0 + e2)
    sel1 = lane == i1
    sel2 = lane == i2
    assign = jnp.where(sel1, 1.0, jnp.where(sel2, 1.0, 0.0))
    r_io = lax.broadcasted_iota(jnp.int32, (tb, tb), 0)
    c_io = lax.broadcasted_iota(jnp.int32, (tb, tb), 1)
    lower = jnp.where(r_io > c_io, 1.0, 0.0).astype(BF16)
    ranks = jnp.dot(lower, assign.astype(BF16), preferred_element_type=F32) + carry_ref[...]
    r1 = jnp.sum(jnp.where(sel1, ranks, 0.0), axis=-1, keepdims=True)
    r2 = jnp.sum(jnp.where(sel2, ranks, 0.0), axis=-1, keepdims=True)
    new_carry = carry_ref[...] + jnp.sum(assign, axis=0, keepdims=True)
    carry_ref[...] = new_carry
    count_ref[...] = new_carry
    meta = jnp.where(lane == 0, i1.astype(F32),
           jnp.where(lane == 1, i2.astype(F32),
           jnp.where(lane == 2, w1,
           jnp.where(lane == 3, w2,
           jnp.where(lane == 4, r1,
           jnp.where(lane == 5, r2, 0.0))))))
    meta_ref[...] = meta


def _router(h, router_pad):
    n = h.shape[0]
    tb = TB_ROUTE
    return pl.pallas_call(
        _router_kernel,
        grid=(n // tb,),
        in_specs=[pl.BlockSpec((tb, D_MODEL), lambda i: (i, 0)),
                  pl.BlockSpec((D_MODEL, LANES), lambda i: (0, 0))],
        out_specs=[pl.BlockSpec((tb, LANES), lambda i: (i, 0)),
                   pl.BlockSpec((1, LANES), lambda i: (0, 0))],
        out_shape=[jax.ShapeDtypeStruct((n, LANES), F32),
                   jax.ShapeDtypeStruct((1, LANES), F32)],
        scratch_shapes=[pltpu.VMEM((1, LANES), F32)],
        compiler_params=_params(("arbitrary",), 32),
        name="router",
    )(h, router_pad)


def _dispatch_kernel(pad_start_ref, pad_len_ref, pos1_ref, pos2_ref, h_hbm, hs_hbm,
                     zero_ref, sem, zsem):
    i = pl.program_id(0)
    tb = pos1_ref.shape[-1]

    @pl.when(i == 0)
    def _():
        zero_ref[...] = jnp.zeros_like(zero_ref)
        zb = zero_ref.shape[0]
        tail_start = pad_start_ref[N_EXPERTS]
        tail_blocks = pad_len_ref[N_EXPERTS] // zb

        def zero_block(r, _):
            pltpu.make_async_copy(zero_ref, hs_hbm.at[pl.ds(tail_start + r * zb, zb)], zsem).start()
            return 0

        def zero_block_wait(r, _):
            pltpu.make_async_copy(zero_ref, hs_hbm.at[pl.ds(0, zb)], zsem).wait()
            return 0

        def zero_row_wait(r, _):
            pltpu.make_async_copy(zero_ref.at[0], hs_hbm.at[0], zsem).wait()
            return 0

        lax.fori_loop(0, tail_blocks, zero_block, 0)
        for e in range(N_EXPERTS):
            def zero_row(r, _, e=e):
                pltpu.make_async_copy(zero_ref.at[0], hs_hbm.at[pad_start_ref[e] + r], zsem).start()
                return 0
            lax.fori_loop(0, pad_len_ref[e], zero_row, 0)
        lax.fori_loop(0, tail_blocks, zero_block_wait, 0)
        for e in range(N_EXPERTS):
            lax.fori_loop(0, pad_len_ref[e], zero_row_wait, 0)

    def move(t, _):
        tok = i * tb + t
        pltpu.make_async_copy(h_hbm.at[tok], hs_hbm.at[pos1_ref[0, 0, t]], sem).start()
        pltpu.make_async_copy(h_hbm.at[tok], hs_hbm.at[pos2_ref[0, 0, t]], sem).start()
        return 0

    lax.fori_loop(0, tb, move, 0)
    pltpu.make_async_copy(h_hbm.at[pl.ds(0, 2 * tb)], hs_hbm.at[pl.ds(0, 2 * tb)], sem).wait()


def _dispatch(pad_start, pad_len, pos1, pos2, h3, p_rows):
    n = h3.shape[0]
    tb = TB_DISPATCH
    sub = h3.shape[1]
    smem_row = pl.BlockSpec((1, 1, tb), lambda i, ps, pn: (i, 0, 0), memory_space=pltpu.SMEM)
    grid_spec = pltpu.PrefetchScalarGridSpec(
        num_scalar_prefetch=2,
        grid=(n // tb,),
        in_specs=[smem_row, smem_row, pl.BlockSpec(memory_space=pl.ANY)],
        out_specs=pl.BlockSpec(memory_space=pl.ANY),
        scratch_shapes=[pltpu.VMEM((TB_DISPATCH, sub, LANES), BF16),
                        pltpu.SemaphoreType.DMA(()), pltpu.SemaphoreType.DMA(())],
    )
    return pl.pallas_call(
        _dispatch_kernel,
        grid_spec=grid_spec,
        out_shape=jax.ShapeDtypeStruct((p_rows, sub, LANES), BF16),
        compiler_params=_params(("arbitrary",), 16),
        name="moe_dispatch",
    )(pad_start, pad_len, pos1.reshape(n // tb, 1, tb), pos2.reshape(n // tb, 1, tb), h3)


def _combine_kernel(pos1_ref, pos2_ref, x_ref, meta_ref, o_hbm, out_ref, buf_ref, sem):
    tb = x_ref.shape[0]

    def fetch(t, _):
        pltpu.make_async_copy(o_hbm.at[pl.ds(pos1_ref[0, 0, t], 1)],
                              buf_ref.at[0, pl.ds(t, 1)], sem).start()
        pltpu.make_async_copy(o_hbm.at[pl.ds(pos2_ref[0, 0, t], 1)],
                              buf_ref.at[1, pl.ds(t, 1)], sem).start()
        return 0

    lax.fori_loop(0, tb, fetch, 0)
    for k in range(2):
        pltpu.make_async_copy(o_hbm.at[pl.ds(0, tb)], buf_ref.at[k], sem).wait()
    w1 = meta_ref[:, 2:3]
    w2 = meta_ref[:, 3:4]
    out_ref[...] = x_ref[...] + w1 * buf_ref[0] + w2 * buf_ref[1]


def _combine(pos1, pos2, x1, meta, o_sorted):
    n = x1.shape[0]
    tb = TB_COMBINE
    smem_row = pl.BlockSpec((1, 1, tb), lambda i: (i, 0, 0), memory_space=pltpu.SMEM)
    return pl.pallas_call(
        _combine_kernel,
        grid=(n // tb,),
        in_specs=[smem_row, smem_row,
                  pl.BlockSpec((tb, D_MODEL), lambda i: (i, 0)),
                  pl.BlockSpec((tb, LANES), lambda i: (i, 0)),
                  pl.BlockSpec(memory_space=pl.ANY)],
        out_specs=pl.BlockSpec((tb, D_MODEL), lambda i: (i, 0)),
        out_shape=jax.ShapeDtypeStruct((n, D_MODEL), F32),
        scratch_shapes=[pltpu.VMEM((2, tb, D_MODEL), F32), pltpu.SemaphoreType.DMA(())],
        compiler_params=_params(("arbitrary",), 32),
        name="moe_combine",
    )(pos1.reshape(n // tb, 1, tb), pos2.reshape(n // tb, 1, tb), x1, meta, o_sorted)


def _moe_layer(x1, h, router_w, we_gate, we_up, we_down):
    n = x1.shape[0]
    tm = TM_FFN
    p_rows = 2 * n + N_EXPERTS * tm
    router_pad = jnp.pad(router_w.astype(BF16), ((0, 0), (0, LANES - N_EXPERTS)))
    meta, counts = _router(h, router_pad)
    counts = counts[0, :N_EXPERTS].astype(jnp.int32)
    padded = ((counts + tm - 1) // tm) * tm
    ends = jnp.cumsum(padded)
    offsets = ends - padded
    e1 = meta[:, 0].astype(jnp.int32)
    e2 = meta[:, 1].astype(jnp.int32)
    pos1 = offsets[e1] + meta[:, 4].astype(jnp.int32)
    pos2 = offsets[e2] + meta[:, 5].astype(jnp.int32)
    n_tiles = (ends[-1] // tm).reshape(1)
    tile_expert = jnp.searchsorted(ends // tm, jnp.arange(p_rows // tm), side="right")
    tile_expert = jnp.minimum(tile_expert, N_EXPERTS - 1).astype(jnp.int32)
    pad_start = jnp.concatenate([offsets + counts, ends[-1:]])
    pad_len = jnp.concatenate([padded - counts, p_rows - ends[-1:]])

    h3 = h.reshape(n, D_MODEL // LANES, LANES)
    hs3 = _dispatch(pad_start, pad_len, pos1, pos2, h3, p_rows)
    hs = hs3.reshape(p_rows, D_MODEL)
    o_sorted = _moe_ffn(tile_expert, n_tiles, hs, we_gate, we_up, we_down)
    return _combine(pos1, pos2, x1, meta, o_sorted)


def _rope_tables(d):
    half = d // 4
    pos = jnp.arange(SEQ, dtype=jnp.int32)
    freqs = ROPE_THETA ** (-jnp.arange(half, dtype=F32) / half)

    def one(p):
        ang = p.astype(F32)[:, None] * freqs[None, :]
        c, s = jnp.cos(ang), jnp.sin(ang)
        return jnp.concatenate([c, c], -1), jnp.concatenate([-s, s], -1)

    cr, sr = one(pos // GRID_W)
    cc, sc = one(pos % GRID_W)
    return jnp.concatenate([cr, cc], -1), jnp.concatenate([sr, sc], -1)


def _pad_row(v, width):
    return jnp.pad(v.astype(F32), (0, width - v.shape[0]))


def kernel(x, mix_norm, w_in, rpb, qn_a, kn_a, qn_b, kn_b, cq_norm, ckv_norm, w_uq, w_ukv, qn_c, kn_c, on_a, on_b, on_c, w_out, ffn_norm, w_gate, w_up, w_down, router, we_gate, we_up, we_down):
    batch, seq, d = x.shape
    assert (seq, d) == (SEQ, D_MODEL)
    n = batch * seq
    depth = w_in.shape[0]
    x2 = x.reshape(n, d)

    cosb, sinb = _rope_tables(HEAD_DIM)
    cos64, sin64 = _rope_tables(C_ROPE)
    cosc = jnp.concatenate([cos64, cos64], -1)
    sinc = jnp.concatenate([sin64, sin64], -1)

    for l in range(depth):
        w_pad = jnp.pad(w_in[l].astype(BF16), ((0, 0), (0, D_IN_PAD - D_IN)))
        wuq = w_uq[l].astype(BF16).reshape(C_Q_RANK, C_HEADS, C_NOPE + C_ROPE)
        wuq = jnp.concatenate([wuq[..., :C_NOPE].reshape(C_Q_RANK, -1),
                               wuq[..., C_NOPE:].reshape(C_Q_RANK, -1)], -1)
        wukv = w_ukv[l].astype(BF16).reshape(C_KV_RANK, C_HEADS, C_NOPE + C_V)
        wukv = jnp.concatenate([wukv[..., :C_NOPE].reshape(C_KV_RANK, -1),
                                wukv[..., C_NOPE:].reshape(C_KV_RANK, -1)], -1)
        gains = jnp.stack([
            _pad_row(qn_a[l], 512), _pad_row(kn_a[l], 512),
            _pad_row(qn_b[l], 512), _pad_row(kn_b[l], 512),
            _pad_row(cq_norm[l], 512), _pad_row(ckv_norm[l], 512),
            _pad_row(qn_c[l][:C_NOPE], 512),
            _pad_row(jnp.concatenate([qn_c[l][C_NOPE:], qn_c[l][C_NOPE:]]), 512),
            _pad_row(kn_c[l][:C_NOPE], 512), _pad_row(kn_c[l][C_NOPE:], 512),
        ] + [jnp.zeros((512,), F32)] * 6)
        qa, ka, va, qb, kb, vb, qc, kc, vc = _in_proj(
            x2, mix_norm[l].reshape(1, d), w_pad, wuq, wukv, gains, cosb, sinb, cosc, sinc)

        oa = _nbr_attention(qa, ka, va, _nbr_bias_tables(rpb[l]), on_a[l].reshape(1, -1))
        ob = _flash_attention(qb, kb, vb, on_b[l].reshape(1, -1), n_heads=B_HEADS,
                              n_kv_heads=B_KV_HEADS, dq=HEAD_DIM, dv=HEAD_DIM,
                              scale=HEAD_DIM ** -0.5)
        oc = _flash_attention(qc, kc, vc, on_c[l].reshape(1, -1), n_heads=C_HEADS,
                              n_kv_heads=C_HEADS, dq=C_QK_PAD, dv=C_V,
                              scale=(C_NOPE + C_ROPE) ** -0.5)

        wo = w_out[l].astype(BF16)
        wa, wb, wc = wo[:A_WIDTH], wo[A_WIDTH:A_WIDTH + B_WIDTH], wo[A_WIDTH + B_WIDTH:]
        g_ffn = ffn_norm[l].reshape(1, d)
        if l % 2 == 0:
            (x1,) = _out_proj(x2, oa, ob, oc, wa, wb, wc, g_ffn, emit_h=False)
            x2 = _dense_ffn(x1, g_ffn, w_gate[l // 2], w_up[l // 2], w_down[l // 2])
        else:
            x1, h = _out_proj(x2, oa, ob, oc, wa, wb, wc, g_ffn, emit_h=True)
            x2 = _moe_layer(x1, h, router[l // 2], we_gate[l // 2], we_up[l // 2],
                            we_down[l // 2])
    return x2.reshape(batch, seq, d)
```

```python
import functools
import math

import jax
import jax.numpy as jnp
import numpy as np
from jax import lax
from jax.experimental import pallas as pl
from jax.experimental.pallas import tpu as pltpu

D_MODEL = 2048
SEQ = 4096
GRID_W = 64
GRID_ROWS = SEQ // GRID_W
HEAD_DIM = 128
A_HEADS = 4
WIN_H = 8
WIN_W = 16
B_HEADS = 8
B_KV_HEADS = 2
C_HEADS = 4
C_Q_RANK = 512
C_KV_RANK = 256
C_NOPE = 128
C_ROPE = 64
C_V = 128
A_WIDTH = A_HEADS * HEAD_DIM
B_WIDTH = B_HEADS * HEAD_DIM
B_KV_WIDTH = B_KV_HEADS * HEAD_DIM
C_WIDTH = C_HEADS * C_V
C_QK_PAD = 256
D_IN = 3 * A_WIDTH + B_WIDTH + 2 * B_KV_WIDTH + C_Q_RANK + C_KV_RANK + C_ROPE
D_IN_PAD = D_IN + 64
N_EXPERTS = 8
ROPE_THETA = 10000.0
EPS = 1e-6
NEG_BIG = -1e30

LANES = 128
MIB = 1024 * 1024

BF16 = jnp.bfloat16
F32 = jnp.float32

TM_IN = 512
TQ_NBR = 4 * GRID_W
TQ_FLASH_B = 256
TQ_FLASH_C = 512
TK_FLASH = 512
TM_OUT = 512
TM_FFN = 1024
TF_FFN = 256
TB_ROUTE = 512
TB_DISPATCH = 256
TB_COMBINE = 256

_TRANS_B = (((1,), (1,)), ((), ()))


def _params(semantics, vmem_mib):
    return pltpu.CompilerParams(dimension_semantics=semantics,
                                vmem_limit_bytes=vmem_mib * MIB)


def _rms(y, gain, n):
    ms = jnp.sum(y * y, axis=-1, keepdims=True) * (1.0 / n)
    return y * lax.rsqrt(ms + EPS) * gain


def _rope(y, cos_t, sin_t, half):
    lane = lax.broadcasted_iota(jnp.int32, y.shape, 1)
    first = (lane & half) == 0
    swapped = jnp.where(first, pltpu.roll(y, LANES - half, 1), pltpu.roll(y, half, 1))
    return y * cos_t + swapped * sin_t


def _in_proj_kernel(x_ref, gmix_ref, w_ref, wuq_ref, wukv_ref, gains_ref,
                    cosb_ref, sinb_ref, cosc_ref, sinc_ref,
                    qa_ref, ka_ref, va_ref, qb_ref, kb_ref, vb_ref,
                    qc_ref, kc_ref, vc_ref):
    x = x_ref[...]
    xn = _rms(x, gmix_ref[...], D_MODEL).astype(BF16)

    def proj(c0, c1):
        return jnp.dot(xn, w_ref[:, c0:c1], preferred_element_type=F32)

    def gain(row, width=LANES):
        return gains_ref[row:row + 1, 0:width]

    cosb, sinb = cosb_ref[...], sinb_ref[...]
    cosc, sinc = cosc_ref[...], sinc_ref[...]

    col = 0
    y = proj(col, col + A_WIDTH)
    for h in range(A_HEADS):
        sl = slice(h * HEAD_DIM, (h + 1) * HEAD_DIM)
        qa_ref[:, sl] = _rms(y[:, sl], gain(0), HEAD_DIM).astype(BF16)
    col += A_WIDTH
    y = proj(col, col + A_WIDTH)
    for h in range(A_HEADS):
        sl = slice(h * HEAD_DIM, (h + 1) * HEAD_DIM)
        ka_ref[:, sl] = _rms(y[:, sl], gain(1), HEAD_DIM).astype(BF16)
    col += A_WIDTH
    va_ref[...] = proj(col, col + A_WIDTH).astype(BF16)
    col += A_WIDTH

    for c in range(B_WIDTH // 512):
        y = proj(col + 512 * c, col + 512 * (c + 1))
        for hh in range(4):
            sl = slice(hh * HEAD_DIM, (hh + 1) * HEAD_DIM)
            out = _rope(_rms(y[:, sl], gain(2), HEAD_DIM), cosb, sinb, 32)
            h = 4 * c + hh
            qb_ref[:, h * HEAD_DIM:(h + 1) * HEAD_DIM] = out.astype(BF16)
    col += B_WIDTH
    y = proj(col, col + 2 * B_KV_WIDTH)
    for h in range(B_KV_HEADS):
        sl = slice(h * HEAD_DIM, (h + 1) * HEAD_DIM)
        kb_ref[:, sl] = _rope(_rms(y[:, sl], gain(3), HEAD_DIM), cosb, sinb, 32).astype(BF16)
    vb_ref[...] = y[:, B_KV_WIDTH:].astype(BF16)
    col += 2 * B_KV_WIDTH

    lane = lax.broadcasted_iota(jnp.int32, (x.shape[0], LANES), 1)
    low = lane < C_ROPE
    cqn = _rms(proj(col, col + C_Q_RANK), gain(4, C_Q_RANK), C_Q_RANK).astype(BF16)
    col += C_Q_RANK
    yq = jnp.dot(cqn, wuq_ref[...], preferred_element_type=F32)
    for p in range(C_HEADS // 2):
        pe2 = yq[:, C_HEADS * C_NOPE + p * LANES:C_HEADS * C_NOPE + (p + 1) * LANES]
        sq = pe2 * pe2
        ms_lo = jnp.sum(jnp.where(low, sq, 0.0), axis=-1, keepdims=True) * (1.0 / C_ROPE)
        ms_hi = jnp.sum(jnp.where(low, 0.0, sq), axis=-1, keepdims=True) * (1.0 / C_ROPE)
        r = jnp.where(low, lax.rsqrt(ms_lo + EPS), lax.rsqrt(ms_hi + EPS))
        roped = _rope(pe2 * r * gain(7), cosc, sinc, 16)
        for hh in range(2):
            h = 2 * p + hh
            nope = _rms(yq[:, h * C_NOPE:(h + 1) * C_NOPE], gain(6), C_NOPE)
            pe = roped if hh == 0 else pltpu.roll(roped, C_ROPE, 1)
            base = h * C_QK_PAD
            qc_ref[:, base:base + C_NOPE] = nope.astype(BF16)
            qc_ref[:, base + C_NOPE:base + C_QK_PAD] = jnp.where(low, pe, 0.0).astype(BF16)

    ckvn = _rms(proj(col, col + C_KV_RANK), gain(5, C_KV_RANK), C_KV_RANK).astype(BF16)
    col += C_KV_RANK
    ykv = jnp.dot(ckvn, wukv_ref[...], preferred_element_type=F32)
    ype = proj(col, col + LANES)
    ms = jnp.sum(ype * ype, axis=-1, keepdims=True) * (1.0 / C_ROPE)
    kpe = _rope(ype * lax.rsqrt(ms + EPS) * gain(9), cosc, sinc, 16)
    kpe = jnp.where(low, kpe, 0.0).astype(BF16)
    for h in range(C_HEADS):
        base = h * C_QK_PAD
        k_nope = _rms(ykv[:, h * C_NOPE:(h + 1) * C_NOPE], gain(8), C_NOPE)
        kc_ref[:, base:base + C_NOPE] = k_nope.astype(BF16)
        kc_ref[:, base + C_NOPE:base + C_QK_PAD] = kpe
    vc_ref[...] = ykv[:, C_HEADS * C_NOPE:].astype(BF16)


def _in_proj(x2, gmix, w_pad, wuq, wukv, gains, cosb, sinb, cosc, sinc):
    n = x2.shape[0]
    tm = TM_IN
    seq_blocks = SEQ // tm
    row = lambda i: (i, 0)
    const = lambda i: (0, 0)
    pos = lambda i: (i % seq_blocks, 0)
    resident = pl.Buffered(1)
    widths = (A_WIDTH, A_WIDTH, A_WIDTH, B_WIDTH, B_KV_WIDTH, B_KV_WIDTH,
              C_HEADS * C_QK_PAD, C_HEADS * C_QK_PAD, C_WIDTH)
    return pl.pallas_call(
        _in_proj_kernel,
        grid=(n // tm,),
        in_specs=[
            pl.BlockSpec((tm, D_MODEL), row),
            pl.BlockSpec((1, D_MODEL), const),
            pl.BlockSpec((D_MODEL, D_IN_PAD), const, pipeline_mode=resident),
            pl.BlockSpec(wuq.shape, const, pipeline_mode=resident),
            pl.BlockSpec(wukv.shape, const, pipeline_mode=resident),
            pl.BlockSpec(gains.shape, const),
            pl.BlockSpec((tm, LANES), pos), pl.BlockSpec((tm, LANES), pos),
            pl.BlockSpec((tm, LANES), pos), pl.BlockSpec((tm, LANES), pos),
        ],
        out_specs=[pl.BlockSpec((tm, w), row) for w in widths],
        out_shape=[jax.ShapeDtypeStruct((n, w), BF16) for w in widths],
        compiler_params=_params(("parallel",), 56),
        name="in_proj",
    )(x2, gmix, w_pad, wuq, wukv, gains, cosb, sinb, cosc, sinc)


def _nbr_kernel(q_ref, k0_ref, k1_ref, k2_ref, v0_ref, v1_ref, v2_ref,
                bias_ref, gain_ref, o_ref):
    scale = HEAD_DIM ** -0.5
    k_refs = (k0_ref, k1_ref, k2_ref)
    v_refs = (v0_ref, v1_ref, v2_ref)
    outs = []
    for h in range(A_HEADS):
        sl = slice(h * HEAD_DIM, (h + 1) * HEAD_DIM)
        q = q_ref[:, sl]
        s = []
        for j in range(3):
            sj = lax.dot_general(q, k_refs[j][:, sl], _TRANS_B, preferred_element_type=F32)
            s.append(sj * scale + bias_ref[h, :, j * TQ_NBR:(j + 1) * TQ_NBR])
        m = jnp.maximum(jnp.maximum(jnp.max(s[0], axis=-1, keepdims=True),
                                    jnp.max(s[1], axis=-1, keepdims=True)),
                        jnp.max(s[2], axis=-1, keepdims=True))
        p = [jnp.exp(sj - m) for sj in s]
        l = (jnp.sum(p[0], axis=-1, keepdims=True) + jnp.sum(p[1], axis=-1, keepdims=True)
             + jnp.sum(p[2], axis=-1, keepdims=True))
        o = jnp.dot(p[0].astype(BF16), v_refs[0][:, sl], preferred_element_type=F32)
        o += jnp.dot(p[1].astype(BF16), v_refs[1][:, sl], preferred_element_type=F32)
        o += jnp.dot(p[2].astype(BF16), v_refs[2][:, sl], preferred_element_type=F32)
        outs.append(o / l)
    o_all = jnp.concatenate(outs, axis=-1)
    o_ref[...] = _rms(o_all, gain_ref[...], A_WIDTH).astype(BF16)


def _nbr_bias_tables(rpb_l):
    heads, n_dr, n_dc = rpb_l.shape
    blocks = GRID_ROWS // 4
    shift = GRID_W - WIN_W
    padded = jnp.pad(rpb_l.astype(F32), ((0, 0), (0, 0), (shift, LANES - shift - n_dc)))
    toep = jnp.broadcast_to(padded[:, :, None, :], (heads, n_dr, GRID_W, LANES))
    toep = toep.reshape(heads, n_dr, GRID_W * LANES)[:, :, :GRID_W * (LANES - 1)]
    toep = toep.reshape(heads, n_dr, GRID_W, LANES - 1)[:, :, :, GRID_W - 1:]
    masked = jnp.full((heads, GRID_W, GRID_W), NEG_BIG, F32)
    classes = []
    for u in (0, 1, blocks - 1):
        start = 4 * min(max(u - 1, 0), blocks - 3)
        per_query_row = []
        for a in range(4):
            qr = 4 * u + a
            r0 = min(max(qr - WIN_H // 2, 0), GRID_ROWS - WIN_H)
            rows = [toep[:, start + i - qr + WIN_H - 1] if r0 <= start + i < r0 + WIN_H else masked
                    for i in range(12)]
            per_query_row.append(jnp.stack(rows, axis=2))
        classes.append(jnp.stack(per_query_row, axis=1))
    table = jnp.stack(classes, axis=1).reshape(heads, 3, TQ_NBR, 3 * TQ_NBR)
    cols = np.arange(GRID_W)
    c0 = np.clip(cols - WIN_W // 2, 0, GRID_W - WIN_W)
    ok_c = (cols[None, :] >= c0[:, None]) & (cols[None, :] < c0[:, None] + WIN_W)
    return jnp.where(np.tile(ok_c, (4, 12)), table, NEG_BIG)


def _nbr_attention(qa, ka, va, bias, gain):
    n = qa.shape[0]
    batch = n // SEQ
    blocks = SEQ // TQ_NBR

    def q_map(b, u):
        return (b * blocks + u, 0)

    def kv_map(j):
        return lambda b, u: (b * blocks + jnp.clip(u - 1, 0, blocks - 3) + j, 0)

    def bias_map(b, u):
        cls = jnp.where(u == 0, 0, jnp.where(u == blocks - 1, 2, 1))
        return (0, cls, 0, 0)

    tile = lambda m: pl.BlockSpec((TQ_NBR, A_WIDTH), m)
    return pl.pallas_call(
        _nbr_kernel,
        grid=(batch, blocks),
        in_specs=[tile(q_map), tile(kv_map(0)), tile(kv_map(1)), tile(kv_map(2)),
                  tile(kv_map(0)), tile(kv_map(1)), tile(kv_map(2)),
                  pl.BlockSpec((A_HEADS, None, TQ_NBR, 3 * TQ_NBR), bias_map),
                  pl.BlockSpec((1, A_WIDTH), lambda b, u: (0, 0))],
        out_specs=tile(q_map),
        out_shape=jax.ShapeDtypeStruct((n, A_WIDTH), BF16),
        compiler_params=_params(("parallel", "arbitrary"), 40),
        name="nbr_attention",
    )(qa, ka, ka, ka, va, va, va, bias, gain)


def _flash_kernel(q_ref, k_ref, v_ref, gain_ref, o_ref,
                  qs_ref, m_ref, l_ref, acc_ref, out_ref, *,
                  n_kv_heads, group, units, dq, dv, tk, exp2_scale):
    tq = q_ref.shape[0]
    n_kv = k_ref.shape[0] // tk
    tiles = tk // LANES
    for first in range(0, n_kv_heads, units):
        for u in range(units):
            for hh in range(group):
                h = (first + u) * group + hh
                qs_ref[u, hh * tq:(hh + 1) * tq, :] = q_ref[:, h * dq:(h + 1) * dq]
        m_ref[...] = jnp.full(m_ref.shape, -jnp.inf, F32)
        l_ref[...] = jnp.zeros(l_ref.shape, F32)
        acc_ref[...] = jnp.zeros(acc_ref.shape, F32)

        def body(j, _, first=first):
            off = pl.multiple_of(j * tk, tk)
            for u in range(units):
                g = first + u
                k = k_ref[pl.ds(off, tk), g * dq:(g + 1) * dq]
                v = v_ref[pl.ds(off, tk), g * dv:(g + 1) * dv]
                s = lax.dot_general(qs_ref[u], k, _TRANS_B, preferred_element_type=F32)
                mx = s[:, 0:LANES]
                for t in range(1, tiles):
                    mx = jnp.maximum(mx, s[:, t * LANES:(t + 1) * LANES])
                m_old = m_ref[u]
                m_new = jnp.maximum(m_old, jnp.max(mx, axis=-1, keepdims=True))
                alpha = jnp.exp2((m_old - m_new) * exp2_scale)
                lsum = alpha * l_ref[u]
                p = []
                for t in range(tiles):
                    pt = jnp.exp2((s[:, t * LANES:(t + 1) * LANES] - m_new) * exp2_scale)
                    lsum = lsum + pt
                    p.append(pt.astype(BF16))
                pv = jnp.dot(jnp.concatenate(p, axis=1), v, preferred_element_type=F32)
                acc_ref[u] = alpha * acc_ref[u] + pv
                m_ref[u] = m_new
                l_ref[u] = lsum
            return 0

        lax.fori_loop(0, n_kv, body, 0)
        for u in range(units):
            o = acc_ref[u] / jnp.sum(l_ref[u], axis=-1, keepdims=True)
            for hh in range(group):
                h = (first + u) * group + hh
                out_ref[:, h * dv:(h + 1) * dv] = o[hh * tq:(hh + 1) * tq]
    width = n_kv_heads * group * dv
    o_ref[...] = _rms(out_ref[...], gain_ref[...], width).astype(BF16)


def _flash_attention(q, k, v, gain, *, n_heads, n_kv_heads, dq, dv, scale, tq, units):
    assert dv == LANES and n_kv_heads % units == 0
    n = q.shape[0]
    batch = n // SEQ
    q_blocks = SEQ // tq
    width = n_heads * dv
    group = n_heads // n_kv_heads
    rows = group * tq
    kernel = functools.partial(
        _flash_kernel, n_kv_heads=n_kv_heads, group=group, units=units, dq=dq, dv=dv,
        tk=TK_FLASH, exp2_scale=scale * math.log2(math.e))
    return pl.pallas_call(
        kernel,
        grid=(batch, q_blocks),
        in_specs=[pl.BlockSpec((tq, n_heads * dq), lambda b, i: (b * q_blocks + i, 0)),
                  pl.BlockSpec((SEQ, n_kv_heads * dq), lambda b, i: (b, 0)),
                  pl.BlockSpec((SEQ, n_kv_heads * dv), lambda b, i: (b, 0)),
                  pl.BlockSpec((1, width), lambda b, i: (0, 0))],
        out_specs=pl.BlockSpec((tq, width), lambda b, i: (b * q_blocks + i, 0)),
        out_shape=jax.ShapeDtypeStruct((n, width), BF16),
        scratch_shapes=[pltpu.VMEM((units, rows, dq), BF16),
                        pltpu.VMEM((units, rows, LANES), F32),
                        pltpu.VMEM((units, rows, LANES), F32),
                        pltpu.VMEM((units, rows, dv), F32),
                        pltpu.VMEM((tq, width), F32)],
        compiler_params=_params(("parallel", "arbitrary"), 48),
        name=f"flash_attention_h{n_heads}",
    )(q, k, v, gain)


def _out_proj_kernel(x_ref, oa_ref, ob_ref, oc_ref, wa_ref, wb_ref, wc_ref, g_ref,
                     x1_ref, *h_ref):
    acc = x_ref[...] + jnp.dot(oa_ref[...], wa_ref[...], preferred_element_type=F32)
    acc += jnp.dot(ob_ref[...], wb_ref[...], preferred_element_type=F32)
    acc += jnp.dot(oc_ref[...], wc_ref[...], preferred_element_type=F32)
    x1_ref[...] = acc
    if h_ref:
        h_ref[0][...] = _rms(acc, g_ref[...], D_MODEL).astype(BF16)


def _out_proj(x2, oa, ob, oc, wa, wb, wc, g_ffn, emit_h):
    n = x2.shape[0]
    tm = TM_OUT
    row = lambda i: (i, 0)
    const = lambda i: (0, 0)
    resident = pl.Buffered(1)
    out_specs = [pl.BlockSpec((tm, D_MODEL), row)]
    out_shape = [jax.ShapeDtypeStruct((n, D_MODEL), F32)]
    if emit_h:
        out_specs.append(pl.BlockSpec((tm, D_MODEL), row))
        out_shape.append(jax.ShapeDtypeStruct((n, D_MODEL), BF16))
    return pl.pallas_call(
        _out_proj_kernel,
        grid=(n // tm,),
        in_specs=[pl.BlockSpec((tm, D_MODEL), row),
                  pl.BlockSpec((tm, A_WIDTH), row),
                  pl.BlockSpec((tm, B_WIDTH), row),
                  pl.BlockSpec((tm, C_WIDTH), row),
                  pl.BlockSpec(wa.shape, const, pipeline_mode=resident),
                  pl.BlockSpec(wb.shape, const, pipeline_mode=resident),
                  pl.BlockSpec(wc.shape, const, pipeline_mode=resident),
                  pl.BlockSpec((1, D_MODEL), const)],
        out_specs=out_specs,
        out_shape=out_shape,
        compiler_params=_params(("parallel",), 48),
        name="out_proj",
    )(x2, oa, ob, oc, wa, wb, wc, g_ffn)


def _swiglu_step(h, wg_ref, wu_ref, wd_ref):
    gate = jnp.dot(h, wg_ref[...].astype(BF16), preferred_element_type=F32)
    up = jnp.dot(h, wu_ref[...].astype(BF16), preferred_element_type=F32)
    act = (gate * jax.nn.sigmoid(gate) * up).astype(BF16)
    return jnp.dot(act, wd_ref[...].astype(BF16), preferred_element_type=F32)


def _dense_ffn_kernel(x_ref, g_ref, wg_ref, wu_ref, wd_ref, o_ref, h_ref):
    @pl.when(pl.program_id(1) == 0)
    def _():
        x = x_ref[...]
        h_ref[...] = _rms(x, g_ref[...], D_MODEL).astype(BF16)
        o_ref[...] = x

    o_ref[...] += _swiglu_step(h_ref[...], wg_ref, wu_ref, wd_ref)


def _dense_ffn(x2, g, w_gate, w_up, w_down):
    n = x2.shape[0]
    d_ff = w_gate.shape[1]
    tm, tf = TM_FFN, TF_FFN
    return pl.pallas_call(
        _dense_ffn_kernel,
        grid=(n // tm, d_ff // tf),
        in_specs=[pl.BlockSpec((tm, D_MODEL), lambda i, f: (i, 0), pipeline_mode=pl.Buffered(1)),
                  pl.BlockSpec((1, D_MODEL), lambda i, f: (0, 0)),
                  pl.BlockSpec((D_MODEL, tf), lambda i, f: (0, f)),
                  pl.BlockSpec((D_MODEL, tf), lambda i, f: (0, f)),
                  pl.BlockSpec((tf, D_MODEL), lambda i, f: (f, 0))],
        out_specs=pl.BlockSpec((tm, D_MODEL), lambda i, f: (i, 0)),
        out_shape=jax.ShapeDtypeStruct((n, D_MODEL), F32),
        scratch_shapes=[pltpu.VMEM((tm, D_MODEL), BF16)],
        compiler_params=_params(("parallel", "arbitrary"), 56),
        name="dense_ffn",
    )(x2, g, w_gate, w_up, w_down)


def _moe_ffn_kernel(tile_expert_ref, n_tiles_ref, h_ref, wg_ref, wu_ref, wd_ref, o_ref):
    i, f = pl.program_id(0), pl.program_id(1)

    @pl.when(i < n_tiles_ref[0])
    def _():
        y = _swiglu_step(h_ref[...], wg_ref, wu_ref, wd_ref)

        @pl.when(f == 0)
        def _():
            o_ref[...] = y

        @pl.when(f > 0)
        def _():
            o_ref[...] += y

    @pl.when(jnp.logical_and(i >= n_tiles_ref[0], f == 0))
    def _():
        o_ref[...] = jnp.zeros_like(o_ref)


def _moe_ffn(tile_expert, n_tiles, hs, we_gate, we_up, we_down):
    p_rows = hs.shape[0]
    d_ff = we_gate.shape[2]
    tm, tf = TM_FFN, TF_FFN
    n_f = d_ff // tf

    def row_map(i, f, te, nt):
        return (jnp.minimum(i, nt[0] - 1), 0)

    def w_col_map(i, f, te, nt):
        valid = i < nt[0]
        return (te[jnp.minimum(i, nt[0] - 1)], 0, jnp.where(valid, f, n_f - 1))

    def w_row_map(i, f, te, nt):
        valid = i < nt[0]
        return (te[jnp.minimum(i, nt[0] - 1)], jnp.where(valid, f, n_f - 1), 0)

    grid_spec = pltpu.PrefetchScalarGridSpec(
        num_scalar_prefetch=2,
        grid=(p_rows // tm, n_f),
        in_specs=[pl.BlockSpec((tm, D_MODEL), row_map),
                  pl.BlockSpec((None, D_MODEL, tf), w_col_map),
                  pl.BlockSpec((None, D_MODEL, tf), w_col_map),
                  pl.BlockSpec((None, tf, D_MODEL), w_row_map)],
        out_specs=pl.BlockSpec((tm, D_MODEL), lambda i, f, te, nt: (i, 0)),
    )
    return pl.pallas_call(
        _moe_ffn_kernel,
        grid_spec=grid_spec,
        out_shape=jax.ShapeDtypeStruct((p_rows, D_MODEL), F32),
        compiler_params=_params(("arbitrary", "arbitrary"), 56),
        name="moe_ffn",
    )(tile_expert, n_tiles, hs, we_gate, we_up, we_down)


def _router_kernel(h_ref, w_ref, meta_ref, count_ref, carry_ref):
    @pl.when(pl.program_id(0) == 0)
    def _():
        carry_ref[...] = jnp.zeros_like(carry_ref)

    tb = h_ref.shape[0]
    logits = jnp.dot(h_ref[...], w_ref[...], preferred_element_type=F32)
    lane = lax.broadcasted_iota(jnp.int32, logits.shape, 1)
    lg = jnp.where(lane < N_EXPERTS, logits, -jnp.inf)
    m1 = jnp.max(lg, axis=-1, keepdims=True)
    i1 = jnp.min(jnp.where(lg == m1, lane, LANES), axis=-1, keepdims=True)
    lg2 = jnp.where(lane == i1, -jnp.inf, lg)
    m2 = jnp.max(lg2, axis=-1, keepdims=True)
    i2 = jnp.min(jnp.where(lg2 == m2, lane, LANES), axis=-1, keepdims=True)
    e2 = jnp.exp(m2 - m1)
    w1 = 1.0 / (1.0 + e2)
    w2 = e2 / (1.0 + e2)
    sel1 = lane == i1
    sel2 = lane == i2
    assign = jnp.where(sel1, 1.0, jnp.where(sel2, 1.0, 0.0))
    r_io = lax.broadcasted_iota(jnp.int32, (tb, tb), 0)
    c_io = lax.broadcasted_iota(jnp.int32, (tb, tb), 1)
    lower = jnp.where(r_io > c_io, 1.0, 0.0).astype(BF16)
    ranks = jnp.dot(lower, assign.astype(BF16), preferred_element_type=F32) + carry_ref[...]
    r1 = jnp.sum(jnp.where(sel1, ranks, 0.0), axis=-1, keepdims=True)
    r2 = jnp.sum(jnp.where(sel2, ranks, 0.0), axis=-1, keepdims=True)
    new_carry = carry_ref[...] + jnp.sum(assign, axis=0, keepdims=True)
    carry_ref[...] = new_carry
    count_ref[...] = new_carry
    meta = jnp.where(lane == 0, i1.astype(F32),
           jnp.where(lane == 1, i2.astype(F32),
           jnp.where(lane == 2, w1,
           jnp.where(lane == 3, w2,
           jnp.where(lane == 4, r1,
           jnp.where(lane == 5, r2, 0.0))))))
    meta_ref[...] = meta


def _router(h, router_pad):
    n = h.shape[0]
    tb = TB_ROUTE
    return pl.pallas_call(
        _router_kernel,
        grid=(n // tb,),
        in_specs=[pl.BlockSpec((tb, D_MODEL), lambda i: (i, 0)),
                  pl.BlockSpec((D_MODEL, LANES), lambda i: (0, 0))],
        out_specs=[pl.BlockSpec((tb, LANES), lambda i: (i, 0)),
                   pl.BlockSpec((1, LANES), lambda i: (0, 0))],
        out_shape=[jax.ShapeDtypeStruct((n, LANES), F32),
                   jax.ShapeDtypeStruct((1, LANES), F32)],
        scratch_shapes=[pltpu.VMEM((1, LANES), F32)],
        compiler_params=_params(("arbitrary",), 32),
        name="router",
    )(h, router_pad)


def _dispatch_kernel(pad_start_ref, pad_len_ref, pos1_ref, pos2_ref, h_ref, hs_hbm,
                     zero_ref, sem, zsem):
    i = pl.program_id(0)
    tb = pos1_ref.shape[-1]

    @pl.when(i == 0)
    def _():
        zero_ref[...] = jnp.zeros_like(zero_ref)
        zb = zero_ref.shape[0]
        tail_start = pad_start_ref[N_EXPERTS]
        tail_blocks = pad_len_ref[N_EXPERTS] // zb

        def zero_block(r, _):
            pltpu.make_async_copy(zero_ref, hs_hbm.at[pl.ds(tail_start + r * zb, zb)], zsem).start()
            return 0

        def zero_block_wait(r, _):
            pltpu.make_async_copy(zero_ref, hs_hbm.at[pl.ds(0, zb)], zsem).wait()
            return 0

        def zero_row_wait(r, _):
            pltpu.make_async_copy(zero_ref.at[0], hs_hbm.at[0], zsem).wait()
            return 0

        lax.fori_loop(0, tail_blocks, zero_block, 0)
        for e in range(N_EXPERTS):
            def zero_row(r, _, e=e):
                pltpu.make_async_copy(zero_ref.at[0], hs_hbm.at[pad_start_ref[e] + r], zsem).start()
                return 0
            lax.fori_loop(0, pad_len_ref[e], zero_row, 0)
        lax.fori_loop(0, tail_blocks, zero_block_wait, 0)
        for e in range(N_EXPERTS):
            lax.fori_loop(0, pad_len_ref[e], zero_row_wait, 0)

    def move(t, _):
        pltpu.make_async_copy(h_ref.at[t], hs_hbm.at[pos1_ref[0, 0, t]], sem).start()
        pltpu.make_async_copy(h_ref.at[t], hs_hbm.at[pos2_ref[0, 0, t]], sem).start()
        return 0

    lax.fori_loop(0, tb, move, 0)
    for _ in range(2):
        pltpu.make_async_copy(h_ref, hs_hbm.at[pl.ds(0, tb)], sem).wait()


def _dispatch(pad_start, pad_len, pos1, pos2, h3, p_rows):
    n = h3.shape[0]
    tb = TB_DISPATCH
    sub = h3.shape[1]
    smem_row = pl.BlockSpec((1, 1, tb), lambda i, ps, pn: (i, 0, 0), memory_space=pltpu.SMEM)
    grid_spec = pltpu.PrefetchScalarGridSpec(
        num_scalar_prefetch=2,
        grid=(n // tb,),
        in_specs=[smem_row, smem_row,
                  pl.BlockSpec((tb, sub, LANES), lambda i, ps, pn: (i, 0, 0))],
        out_specs=pl.BlockSpec(memory_space=pl.ANY),
        scratch_shapes=[pltpu.VMEM((TB_DISPATCH, sub, LANES), BF16),
                        pltpu.SemaphoreType.DMA(()), pltpu.SemaphoreType.DMA(())],
    )
    return pl.pallas_call(
        _dispatch_kernel,
        grid_spec=grid_spec,
        out_shape=jax.ShapeDtypeStruct((p_rows, sub, LANES), BF16),
        compiler_params=_params(("arbitrary",), 16),
        name="moe_dispatch",
    )(pad_start, pad_len, pos1.reshape(n // tb, 1, tb), pos2.reshape(n // tb, 1, tb), h3)


def _combine_kernel(pos1_ref, pos2_ref, x_ref, meta_ref, o_hbm, out_ref, buf_ref, sem):
    tb = x_ref.shape[0]

    def fetch(t, _):
        pltpu.make_async_copy(o_hbm.at[pl.ds(pos1_ref[0, 0, t], 1)],
                              buf_ref.at[0, pl.ds(t, 1)], sem).start()
        pltpu.make_async_copy(o_hbm.at[pl.ds(pos2_ref[0, 0, t], 1)],
                              buf_ref.at[1, pl.ds(t, 1)], sem).start()
        return 0

    lax.fori_loop(0, tb, fetch, 0)
    for k in range(2):
        pltpu.make_async_copy(o_hbm.at[pl.ds(0, tb)], buf_ref.at[k], sem).wait()
    w1 = meta_ref[:, 2:3]
    w2 = meta_ref[:, 3:4]
    out_ref[...] = x_ref[...] + w1 * buf_ref[0] + w2 * buf_ref[1]


def _combine(pos1, pos2, x1, meta, o_sorted):
    n = x1.shape[0]
    tb = TB_COMBINE
    smem_row = pl.BlockSpec((1, 1, tb), lambda i: (i, 0, 0), memory_space=pltpu.SMEM)
    return pl.pallas_call(
        _combine_kernel,
        grid=(n // tb,),
        in_specs=[smem_row, smem_row,
                  pl.BlockSpec((tb, D_MODEL), lambda i: (i, 0)),
                  pl.BlockSpec((tb, LANES), lambda i: (i, 0)),
                  pl.BlockSpec(memory_space=pl.ANY)],
        out_specs=pl.BlockSpec((tb, D_MODEL), lambda i: (i, 0)),
        out_shape=jax.ShapeDtypeStruct((n, D_MODEL), F32),
        scratch_shapes=[pltpu.VMEM((2, tb, D_MODEL), F32), pltpu.SemaphoreType.DMA(())],
        compiler_params=_params(("arbitrary",), 32),
        name="moe_combine",
    )(pos1.reshape(n // tb, 1, tb), pos2.reshape(n // tb, 1, tb), x1, meta, o_sorted)


def _moe_layer(x1, h, router_w, we_gate, we_up, we_down):
    n = x1.shape[0]
    tm = TM_FFN
    p_rows = 2 * n + N_EXPERTS * tm
    router_pad = jnp.pad(router_w.astype(BF16), ((0, 0), (0, LANES - N_EXPERTS)))
    meta, counts = _router(h, router_pad)
    counts = counts[0, :N_EXPERTS].astype(jnp.int32)
    padded = ((counts + tm - 1) // tm) * tm
    ends = jnp.cumsum(padded)
    offsets = ends - padded
    e1 = meta[:, 0].astype(jnp.int32)
    e2 = meta[:, 1].astype(jnp.int32)
    pos1 = offsets[e1] + meta[:, 4].astype(jnp.int32)
    pos2 = offsets[e2] + meta[:, 5].astype(jnp.int32)
    n_tiles = (ends[-1] // tm).reshape(1)
    tile_ids = jnp.arange(p_rows // tm, dtype=jnp.int32)
    tile_expert = jnp.sum((tile_ids[:, None] >= (ends // tm)[None, :]).astype(jnp.int32), axis=1)
    tile_expert = jnp.minimum(tile_expert, N_EXPERTS - 1)
    pad_start = jnp.concatenate([offsets + counts, ends[-1:]])
    pad_len = jnp.concatenate([padded - counts, p_rows - ends[-1:]])

    h3 = h.reshape(n, D_MODEL // LANES, LANES)
    hs3 = _dispatch(pad_start, pad_len, pos1, pos2, h3, p_rows)
    hs = hs3.reshape(p_rows, D_MODEL)
    o_sorted = _moe_ffn(tile_expert, n_tiles, hs, we_gate, we_up, we_down)
    return _combine(pos1, pos2, x1, meta, o_sorted)


def _rope_tables(d):
    half = d // 4
    pos = jnp.arange(SEQ, dtype=jnp.int32)
    freqs = ROPE_THETA ** (-jnp.arange(half, dtype=F32) / half)

    def one(p):
        ang = p.astype(F32)[:, None] * freqs[None, :]
        c, s = jnp.cos(ang), jnp.sin(ang)
        return jnp.concatenate([c, c], -1), jnp.concatenate([-s, s], -1)

    cr, sr = one(pos // GRID_W)
    cc, sc = one(pos % GRID_W)
    return jnp.concatenate([cr, cc], -1), jnp.concatenate([sr, sc], -1)


def _pad_row(v, width):
    return jnp.pad(v.astype(F32), (0, width - v.shape[0]))


def kernel(x, mix_norm, w_in, rpb, qn_a, kn_a, qn_b, kn_b, cq_norm, ckv_norm, w_uq, w_ukv, qn_c, kn_c, on_a, on_b, on_c, w_out, ffn_norm, w_gate, w_up, w_down, router, we_gate, we_up, we_down):
    batch, seq, d = x.shape
    assert (seq, d) == (SEQ, D_MODEL)
    n = batch * seq
    depth = w_in.shape[0]
    x2 = x.reshape(n, d)

    cosb, sinb = _rope_tables(HEAD_DIM)
    cos64, sin64 = _rope_tables(C_ROPE)
    cosc = jnp.concatenate([cos64, cos64], -1)
    sinc = jnp.concatenate([sin64, sin64], -1)

    for l in range(depth):
        w_pad = jnp.pad(w_in[l].astype(BF16), ((0, 0), (0, D_IN_PAD - D_IN)))
        wuq = w_uq[l].astype(BF16).reshape(C_Q_RANK, C_HEADS, C_NOPE + C_ROPE)
        wuq = jnp.concatenate([wuq[..., :C_NOPE].reshape(C_Q_RANK, -1),
                               wuq[..., C_NOPE:].reshape(C_Q_RANK, -1)], -1)
        wukv = w_ukv[l].astype(BF16).reshape(C_KV_RANK, C_HEADS, C_NOPE + C_V)
        wukv = jnp.concatenate([wukv[..., :C_NOPE].reshape(C_KV_RANK, -1),
                                wukv[..., C_NOPE:].reshape(C_KV_RANK, -1)], -1)
        gains = jnp.stack([
            _pad_row(qn_a[l], 512), _pad_row(kn_a[l], 512),
            _pad_row(qn_b[l], 512), _pad_row(kn_b[l], 512),
            _pad_row(cq_norm[l], 512), _pad_row(ckv_norm[l], 512),
            _pad_row(qn_c[l][:C_NOPE], 512),
            _pad_row(jnp.concatenate([qn_c[l][C_NOPE:], qn_c[l][C_NOPE:]]), 512),
            _pad_row(kn_c[l][:C_NOPE], 512), _pad_row(kn_c[l][C_NOPE:], 512),
        ] + [jnp.zeros((512,), F32)] * 6)
        qa, ka, va, qb, kb, vb, qc, kc, vc = _in_proj(
            x2, mix_norm[l].reshape(1, d), w_pad, wuq, wukv, gains, cosb, sinb, cosc, sinc)

        oa = _nbr_attention(qa, ka, va, _nbr_bias_tables(rpb[l]), on_a[l].reshape(1, -1))
        ob = _flash_attention(qb, kb, vb, on_b[l].reshape(1, -1), n_heads=B_HEADS,
                              n_kv_heads=B_KV_HEADS, dq=HEAD_DIM, dv=HEAD_DIM,
                              scale=HEAD_DIM ** -0.5, tq=TQ_FLASH_B, units=1)
        oc = _flash_attention(qc, kc, vc, on_c[l].reshape(1, -1), n_heads=C_HEADS,
                              n_kv_heads=C_HEADS, dq=C_QK_PAD, dv=C_V,
                              scale=(C_NOPE + C_ROPE) ** -0.5, tq=TQ_FLASH_C, units=2)

        wo = w_out[l].astype(BF16)
        wa, wb, wc = wo[:A_WIDTH], wo[A_WIDTH:A_WIDTH + B_WIDTH], wo[A_WIDTH + B_WIDTH:]
        g_ffn = ffn_norm[l].reshape(1, d)
        if l % 2 == 0:
            (x1,) = _out_proj(x2, oa, ob, oc, wa, wb, wc, g_ffn, emit_h=False)
            x2 = _dense_ffn(x1, g_ffn, w_gate[l // 2], w_up[l // 2], w_down[l // 2])
        else:
            x1, h = _out_proj(x2, oa, ob, oc, wa, wb, wc, g_ffn, emit_h=True)
            x2 = _moe_layer(x1, h, router[l // 2], we_gate[l // 2], we_up[l // 2],
                            we_down[l // 2])
    return x2.reshape(batch, seq, d)
```

```python
import functools
import math

import jax
import jax.numpy as jnp
import numpy as np
from jax import lax
from jax.experimental import pallas as pl
from jax.experimental.pallas import tpu as pltpu

D_MODEL = 2048
SEQ = 4096
GRID_W = 64
GRID_ROWS = SEQ // GRID_W
HEAD_DIM = 128
A_HEADS = 4
WIN_H = 8
WIN_W = 16
B_HEADS = 8
B_KV_HEADS = 2
C_HEADS = 4
C_Q_RANK = 512
C_KV_RANK = 256
C_NOPE = 128
C_ROPE = 64
C_V = 128
A_WIDTH = A_HEADS * HEAD_DIM
B_WIDTH = B_HEADS * HEAD_DIM
B_KV_WIDTH = B_KV_HEADS * HEAD_DIM
C_WIDTH = C_HEADS * C_V
C_QK_PAD = 256
D_IN = 3 * A_WIDTH + B_WIDTH + 2 * B_KV_WIDTH + C_Q_RANK + C_KV_RANK + C_ROPE
D_IN_PAD = D_IN + 64
N_EXPERTS = 8
ROPE_THETA = 10000.0
EPS = 1e-6
NEG_BIG = -1e30

LANES = 128
MIB = 1024 * 1024

BF16 = jnp.bfloat16
F32 = jnp.float32

TM_IN = 512
TQ_NBR = 4 * GRID_W
TQ_FLASH_B = 256
TQ_FLASH_C = 512
TK_FLASH = 512
TM_OUT = 512
TM_FFN = 1024
TF_FFN = 256
TB_ROUTE = 512
TB_DISPATCH = 256
TB_COMBINE = 256

_TRANS_B = (((1,), (1,)), ((), ()))


def _params(semantics, vmem_mib):
    return pltpu.CompilerParams(dimension_semantics=semantics,
                                vmem_limit_bytes=vmem_mib * MIB)


def _rms(y, gain, n):
    ms = jnp.sum(y * y, axis=-1, keepdims=True) * (1.0 / n)
    return y * lax.rsqrt(ms + EPS) * gain


def _rope(y, cos_t, sin_t, half):
    lane = lax.broadcasted_iota(jnp.int32, y.shape, 1)
    first = (lane & half) == 0
    swapped = jnp.where(first, pltpu.roll(y, LANES - half, 1), pltpu.roll(y, half, 1))
    return y * cos_t + swapped * sin_t


def _in_proj_kernel(x_ref, gmix_ref, w_ref, wuq_ref, wukv_ref, gains_ref,
                    cosb_ref, sinb_ref, cosc_ref, sinc_ref,
                    qa_ref, ka_ref, va_ref, qb_ref, kb_ref, vb_ref,
                    qc_ref, kc_ref, vc_ref):
    x = x_ref[...]
    xn = _rms(x, gmix_ref[...], D_MODEL).astype(BF16)

    def proj(c0, c1):
        return jnp.dot(xn, w_ref[:, c0:c1], preferred_element_type=F32)

    def gain(row, width=LANES):
        return gains_ref[row:row + 1, 0:width]

    cosb, sinb = cosb_ref[...], sinb_ref[...]
    cosc, sinc = cosc_ref[...], sinc_ref[...]

    col = 0
    y = proj(col, col + A_WIDTH)
    for h in range(A_HEADS):
        sl = slice(h * HEAD_DIM, (h + 1) * HEAD_DIM)
        qa_ref[:, sl] = _rms(y[:, sl], gain(0), HEAD_DIM).astype(BF16)
    col += A_WIDTH
    y = proj(col, col + A_WIDTH)
    for h in range(A_HEADS):
        sl = slice(h * HEAD_DIM, (h + 1) * HEAD_DIM)
        ka_ref[:, sl] = _rms(y[:, sl], gain(1), HEAD_DIM).astype(BF16)
    col += A_WIDTH
    va_ref[...] = proj(col, col + A_WIDTH).astype(BF16)
    col += A_WIDTH

    for c in range(B_WIDTH // 512):
        y = proj(col + 512 * c, col + 512 * (c + 1))
        for hh in range(4):
            sl = slice(hh * HEAD_DIM, (hh + 1) * HEAD_DIM)
            out = _rope(_rms(y[:, sl], gain(2), HEAD_DIM), cosb, sinb, 32)
            h = 4 * c + hh
            qb_ref[:, h * HEAD_DIM:(h + 1) * HEAD_DIM] = out.astype(BF16)
    col += B_WIDTH
    y = proj(col, col + 2 * B_KV_WIDTH)
    for h in range(B_KV_HEADS):
        sl = slice(h * HEAD_DIM, (h + 1) * HEAD_DIM)
        kb_ref[:, sl] = _rope(_rms(y[:, sl], gain(3), HEAD_DIM), cosb, sinb, 32).astype(BF16)
    vb_ref[...] = y[:, B_KV_WIDTH:].astype(BF16)
    col += 2 * B_KV_WIDTH

    lane = lax.broadcasted_iota(jnp.int32, (x.shape[0], LANES), 1)
    low = lane < C_ROPE
    cqn = _rms(proj(col, col + C_Q_RANK), gain(4, C_Q_RANK), C_Q_RANK).astype(BF16)
    col += C_Q_RANK
    yq = jnp.dot(cqn, wuq_ref[...], preferred_element_type=F32)
    for p in range(C_HEADS // 2):
        pe2 = yq[:, C_HEADS * C_NOPE + p * LANES:C_HEADS * C_NOPE + (p + 1) * LANES]
        sq = pe2 * pe2
        ms_lo = jnp.sum(jnp.where(low, sq, 0.0), axis=-1, keepdims=True) * (1.0 / C_ROPE)
        ms_hi = jnp.sum(jnp.where(low, 0.0, sq), axis=-1, keepdims=True) * (1.0 / C_ROPE)
        r = jnp.where(low, lax.rsqrt(ms_lo + EPS), lax.rsqrt(ms_hi + EPS))
        roped = _rope(pe2 * r * gain(7), cosc, sinc, 16)
        for hh in range(2):
            h = 2 * p + hh
            nope = _rms(yq[:, h * C_NOPE:(h + 1) * C_NOPE], gain(6), C_NOPE)
            pe = roped if hh == 0 else pltpu.roll(roped, C_ROPE, 1)
            base = h * C_QK_PAD
            qc_ref[:, base:base + C_NOPE] = nope.astype(BF16)
            qc_ref[:, base + C_NOPE:base + C_QK_PAD] = jnp.where(low, pe, 0.0).astype(BF16)

    ckvn = _rms(proj(col, col + C_KV_RANK), gain(5, C_KV_RANK), C_KV_RANK).astype(BF16)
    col += C_KV_RANK
    ykv = jnp.dot(ckvn, wukv_ref[...], preferred_element_type=F32)
    ype = proj(col, col + LANES)
    ms = jnp.sum(ype * ype, axis=-1, keepdims=True) * (1.0 / C_ROPE)
    kpe = _rope(ype * lax.rsqrt(ms + EPS) * gain(9), cosc, sinc, 16)
    kpe = jnp.where(low, kpe, 0.0).astype(BF16)
    for h in range(C_HEADS):
        base = h * C_QK_PAD
        k_nope = _rms(ykv[:, h * C_NOPE:(h + 1) * C_NOPE], gain(8), C_NOPE)
        kc_ref[:, base:base + C_NOPE] = k_nope.astype(BF16)
        kc_ref[:, base + C_NOPE:base + C_QK_PAD] = kpe
    vc_ref[...] = ykv[:, C_HEADS * C_NOPE:].astype(BF16)


def _in_proj(x2, gmix, w_pad, wuq, wukv, gains, cosb, sinb, cosc, sinc):
    n = x2.shape[0]
    tm = TM_IN
    seq_blocks = SEQ // tm
    row = lambda i: (i, 0)
    const = lambda i: (0, 0)
    pos = lambda i: (i % seq_blocks, 0)
    resident = pl.Buffered(1)
    widths = (A_WIDTH, A_WIDTH, A_WIDTH, B_WIDTH, B_KV_WIDTH, B_KV_WIDTH,
              C_HEADS * C_QK_PAD, C_HEADS * C_QK_PAD, C_WIDTH)
    return pl.pallas_call(
        _in_proj_kernel,
        grid=(n // tm,),
        in_specs=[
            pl.BlockSpec((tm, D_MODEL), row),
            pl.BlockSpec((1, D_MODEL), const),
            pl.BlockSpec((D_MODEL, D_IN_PAD), const, pipeline_mode=resident),
            pl.BlockSpec(wuq.shape, const, pipeline_mode=resident),
            pl.BlockSpec(wukv.shape, const, pipeline_mode=resident),
            pl.BlockSpec(gains.shape, const),
            pl.BlockSpec((tm, LANES), pos), pl.BlockSpec((tm, LANES), pos),
            pl.BlockSpec((tm, LANES), pos), pl.BlockSpec((tm, LANES), pos),
        ],
        out_specs=[pl.BlockSpec((tm, w), row) for w in widths],
        out_shape=[jax.ShapeDtypeStruct((n, w), BF16) for w in widths],
        compiler_params=_params(("parallel",), 56),
        name="in_proj",
    )(x2, gmix, w_pad, wuq, wukv, gains, cosb, sinb, cosc, sinc)


def _nbr_kernel(q_ref, k0_ref, k1_ref, k2_ref, v0_ref, v1_ref, v2_ref,
                bias_ref, gain_ref, o_ref):
    scale = HEAD_DIM ** -0.5
    k_refs = (k0_ref, k1_ref, k2_ref)
    v_refs = (v0_ref, v1_ref, v2_ref)
    outs = []
    for h in range(A_HEADS):
        sl = slice(h * HEAD_DIM, (h + 1) * HEAD_DIM)
        q = q_ref[:, sl]
        s = []
        for j in range(3):
            sj = lax.dot_general(q, k_refs[j][:, sl], _TRANS_B, preferred_element_type=F32)
            s.append(sj * scale + bias_ref[h, :, j * TQ_NBR:(j + 1) * TQ_NBR])
        m = jnp.maximum(jnp.maximum(jnp.max(s[0], axis=-1, keepdims=True),
                                    jnp.max(s[1], axis=-1, keepdims=True)),
                        jnp.max(s[2], axis=-1, keepdims=True))
        p = [jnp.exp(sj - m) for sj in s]
        l = (jnp.sum(p[0], axis=-1, keepdims=True) + jnp.sum(p[1], axis=-1, keepdims=True)
             + jnp.sum(p[2], axis=-1, keepdims=True))
        o = jnp.dot(p[0].astype(BF16), v_refs[0][:, sl], preferred_element_type=F32)
        o += jnp.dot(p[1].astype(BF16), v_refs[1][:, sl], preferred_element_type=F32)
        o += jnp.dot(p[2].astype(BF16), v_refs[2][:, sl], preferred_element_type=F32)
        outs.append(o / l)
    o_all = jnp.concatenate(outs, axis=-1)
    o_ref[...] = _rms(o_all, gain_ref[...], A_WIDTH).astype(BF16)


def _nbr_bias_tables(rpb_l):
    heads, n_dr, n_dc = rpb_l.shape
    blocks = GRID_ROWS // 4
    shift = GRID_W - WIN_W
    padded = jnp.pad(rpb_l.astype(F32), ((0, 0), (0, 0), (shift, LANES - shift - n_dc)))
    toep = jnp.broadcast_to(padded[:, :, None, :], (heads, n_dr, GRID_W, LANES))
    toep = toep.reshape(heads, n_dr, GRID_W * LANES)[:, :, :GRID_W * (LANES - 1)]
    toep = toep.reshape(heads, n_dr, GRID_W, LANES - 1)[:, :, :, GRID_W - 1:]
    masked = jnp.full((heads, GRID_W, GRID_W), NEG_BIG, F32)
    classes = []
    for u in (0, 1, blocks - 1):
        start = 4 * min(max(u - 1, 0), blocks - 3)
        per_query_row = []
        for a in range(4):
            qr = 4 * u + a
            r0 = min(max(qr - WIN_H // 2, 0), GRID_ROWS - WIN_H)
            rows = [toep[:, start + i - qr + WIN_H - 1] if r0 <= start + i < r0 + WIN_H else masked
                    for i in range(12)]
            per_query_row.append(jnp.stack(rows, axis=2))
        classes.append(jnp.stack(per_query_row, axis=1))
    table = jnp.stack(classes, axis=1).reshape(heads, 3, TQ_NBR, 3 * TQ_NBR)
    cols = np.arange(GRID_W)
    c0 = np.clip(cols - WIN_W // 2, 0, GRID_W - WIN_W)
    ok_c = (cols[None, :] >= c0[:, None]) & (cols[None, :] < c0[:, None] + WIN_W)
    return jnp.where(np.tile(ok_c, (4, 12)), table, NEG_BIG)


def _nbr_attention(qa, ka, va, bias, gain):
    n = qa.shape[0]
    batch = n // SEQ
    blocks = SEQ // TQ_NBR

    def q_map(b, u):
        return (b * blocks + u, 0)

    def kv_map(j):
        return lambda b, u: (b * blocks + jnp.clip(u - 1, 0, blocks - 3) + j, 0)

    def bias_map(b, u):
        cls = jnp.where(u == 0, 0, jnp.where(u == blocks - 1, 2, 1))
        return (0, cls, 0, 0)

    tile = lambda m: pl.BlockSpec((TQ_NBR, A_WIDTH), m)
    return pl.pallas_call(
        _nbr_kernel,
        grid=(batch, blocks),
        in_specs=[tile(q_map), tile(kv_map(0)), tile(kv_map(1)), tile(kv_map(2)),
                  tile(kv_map(0)), tile(kv_map(1)), tile(kv_map(2)),
                  pl.BlockSpec((A_HEADS, None, TQ_NBR, 3 * TQ_NBR), bias_map),
                  pl.BlockSpec((1, A_WIDTH), lambda b, u: (0, 0))],
        out_specs=tile(q_map),
        out_shape=jax.ShapeDtypeStruct((n, A_WIDTH), BF16),
        compiler_params=_params(("parallel", "arbitrary"), 40),
        name="nbr_attention",
    )(qa, ka, ka, ka, va, va, va, bias, gain)


def _flash_kernel(q_ref, k_ref, v_ref, gain_ref, o_ref,
                  qs_ref, m_ref, l_ref, acc_ref, out_ref, *,
                  n_kv_heads, group, units, dq, dv, tk, exp2_scale):
    tq = q_ref.shape[0]
    n_kv = k_ref.shape[0] // tk
    tiles = tk // LANES
    for first in range(0, n_kv_heads, units):
        for u in range(units):
            for hh in range(group):
                h = (first + u) * group + hh
                qs_ref[u, hh * tq:(hh + 1) * tq, :] = q_ref[:, h * dq:(h + 1) * dq]
        m_ref[...] = jnp.full(m_ref.shape, -jnp.inf, F32)
        l_ref[...] = jnp.zeros(l_ref.shape, F32)
        acc_ref[...] = jnp.zeros(acc_ref.shape, F32)

        def body(j, _, first=first):
            off = pl.multiple_of(j * tk, tk)
            for u in range(units):
                g = first + u
                k = k_ref[pl.ds(off, tk), g * dq:(g + 1) * dq]
                v = v_ref[pl.ds(off, tk), g * dv:(g + 1) * dv]
                s = lax.dot_general(qs_ref[u], k, _TRANS_B, preferred_element_type=F32)
                mx = s[:, 0:LANES]
                for t in range(1, tiles):
                    mx = jnp.maximum(mx, s[:, t * LANES:(t + 1) * LANES])
                m_old = m_ref[u]
                m_new = jnp.maximum(m_old, jnp.max(mx, axis=-1, keepdims=True))
                alpha = jnp.exp2((m_old - m_new) * exp2_scale)
                lsum = alpha * l_ref[u]
                p = []
                for t in range(tiles):
                    pt = jnp.exp2((s[:, t * LANES:(t + 1) * LANES] - m_new) * exp2_scale)
                    lsum = lsum + pt
                    p.append(pt.astype(BF16))
                pv = jnp.dot(jnp.concatenate(p, axis=1), v, preferred_element_type=F32)
                acc_ref[u] = alpha * acc_ref[u] + pv
                m_ref[u] = m_new
                l_ref[u] = lsum
            return 0

        lax.fori_loop(0, n_kv, body, 0)
        for u in range(units):
            o = acc_ref[u] / jnp.sum(l_ref[u], axis=-1, keepdims=True)
            for hh in range(group):
                h = (first + u) * group + hh
                out_ref[:, h * dv:(h + 1) * dv] = o[hh * tq:(hh + 1) * tq]
    width = n_kv_heads * group * dv
    o_ref[...] = _rms(out_ref[...], gain_ref[...], width).astype(BF16)


def _flash_attention(q, k, v, gain, *, n_heads, n_kv_heads, dq, dv, scale, tq, units):
    assert dv == LANES and n_kv_heads % units == 0
    n = q.shape[0]
    batch = n // SEQ
    q_blocks = SEQ // tq
    width = n_heads * dv
    group = n_heads // n_kv_heads
    rows = group * tq
    kernel = functools.partial(
        _flash_kernel, n_kv_heads=n_kv_heads, group=group, units=units, dq=dq, dv=dv,
        tk=TK_FLASH, exp2_scale=scale * math.log2(math.e))
    return pl.pallas_call(
        kernel,
        grid=(batch, q_blocks),
        in_specs=[pl.BlockSpec((tq, n_heads * dq), lambda b, i: (b * q_blocks + i, 0)),
                  pl.BlockSpec((SEQ, n_kv_heads * dq), lambda b, i: (b, 0)),
                  pl.BlockSpec((SEQ, n_kv_heads * dv), lambda b, i: (b, 0)),
                  pl.BlockSpec((1, width), lambda b, i: (0, 0))],
        out_specs=pl.BlockSpec((tq, width), lambda b, i: (b * q_blocks + i, 0)),
        out_shape=jax.ShapeDtypeStruct((n, width), BF16),
        scratch_shapes=[pltpu.VMEM((units, rows, dq), BF16),
                        pltpu.VMEM((units, rows, LANES), F32),
                        pltpu.VMEM((units, rows, LANES), F32),
                        pltpu.VMEM((units, rows, dv), F32),
                        pltpu.VMEM((tq, width), F32)],
        compiler_params=_params(("parallel", "arbitrary"), 48),
        name=f"flash_attention_h{n_heads}",
    )(q, k, v, gain)


def _out_proj_kernel(x_ref, oa_ref, ob_ref, oc_ref, wa_ref, wb_ref, wc_ref, g_ref,
                     x1_ref, *h_ref):
    acc = x_ref[...] + jnp.dot(oa_ref[...], wa_ref[...], preferred_element_type=F32)
    acc += jnp.dot(ob_ref[...], wb_ref[...], preferred_element_type=F32)
    acc += jnp.dot(oc_ref[...], wc_ref[...], preferred_element_type=F32)
    x1_ref[...] = acc
    if h_ref:
        h_ref[0][...] = _rms(acc, g_ref[...], D_MODEL).astype(BF16)


def _out_proj(x2, oa, ob, oc, wa, wb, wc, g_ffn, emit_h):
    n = x2.shape[0]
    tm = TM_OUT
    row = lambda i: (i, 0)
    const = lambda i: (0, 0)
    resident = pl.Buffered(1)
    out_specs = [pl.BlockSpec((tm, D_MODEL), row)]
    out_shape = [jax.ShapeDtypeStruct((n, D_MODEL), F32)]
    if emit_h:
        out_specs.append(pl.BlockSpec((tm, D_MODEL), row))
        out_shape.append(jax.ShapeDtypeStruct((n, D_MODEL), BF16))
    return pl.pallas_call(
        _out_proj_kernel,
        grid=(n // tm,),
        in_specs=[pl.BlockSpec((tm, D_MODEL), row),
                  pl.BlockSpec((tm, A_WIDTH), row),
                  pl.BlockSpec((tm, B_WIDTH), row),
                  pl.BlockSpec((tm, C_WIDTH), row),
                  pl.BlockSpec(wa.shape, const, pipeline_mode=resident),
                  pl.BlockSpec(wb.shape, const, pipeline_mode=resident),
                  pl.BlockSpec(wc.shape, const, pipeline_mode=resident),
                  pl.BlockSpec((1, D_MODEL), const)],
        out_specs=out_specs,
        out_shape=out_shape,
        compiler_params=_params(("parallel",), 48),
        name="out_proj",
    )(x2, oa, ob, oc, wa, wb, wc, g_ffn)


def _swiglu_step(h, wg_ref, wu_ref, wd_ref):
    gate = jnp.dot(h, wg_ref[...].astype(BF16), preferred_element_type=F32)
    up = jnp.dot(h, wu_ref[...].astype(BF16), preferred_element_type=F32)
    act = (gate * jax.nn.sigmoid(gate) * up).astype(BF16)
    return jnp.dot(act, wd_ref[...].astype(BF16), preferred_element_type=F32)


def _dense_ffn_kernel(x_ref, g_ref, wg_ref, wu_ref, wd_ref, o_ref, h_ref):
    @pl.when(pl.program_id(1) == 0)
    def _():
        x = x_ref[...]
        h_ref[...] = _rms(x, g_ref[...], D_MODEL).astype(BF16)
        o_ref[...] = x

    o_ref[...] += _swiglu_step(h_ref[...], wg_ref, wu_ref, wd_ref)


def _dense_ffn(x2, g, w_gate, w_up, w_down):
    n = x2.shape[0]
    d_ff = w_gate.shape[1]
    tm, tf = TM_FFN, TF_FFN
    return pl.pallas_call(
        _dense_ffn_kernel,
        grid=(n // tm, d_ff // tf),
        in_specs=[pl.BlockSpec((tm, D_MODEL), lambda i, f: (i, 0), pipeline_mode=pl.Buffered(1)),
                  pl.BlockSpec((1, D_MODEL), lambda i, f: (0, 0)),
                  pl.BlockSpec((D_MODEL, tf), lambda i, f: (0, f)),
                  pl.BlockSpec((D_MODEL, tf), lambda i, f: (0, f)),
                  pl.BlockSpec((tf, D_MODEL), lambda i, f: (f, 0))],
        out_specs=pl.BlockSpec((tm, D_MODEL), lambda i, f: (i, 0)),
        out_shape=jax.ShapeDtypeStruct((n, D_MODEL), F32),
        scratch_shapes=[pltpu.VMEM((tm, D_MODEL), BF16)],
        compiler_params=_params(("parallel", "arbitrary"), 56),
        name="dense_ffn",
    )(x2, g, w_gate, w_up, w_down)


def _moe_ffn_kernel(tile_expert_ref, n_tiles_ref, h_ref, wg_ref, wu_ref, wd_ref, o_ref):
    i, f = pl.program_id(0), pl.program_id(1)

    @pl.when(f == 0)
    def _():
        o_ref[...] = jnp.zeros_like(o_ref)

    @pl.when(i < n_tiles_ref[0])
    def _():
        o_ref[...] += _swiglu_step(h_ref[...], wg_ref, wu_ref, wd_ref)


def _moe_ffn(tile_expert, n_tiles, hs, we_gate, we_up, we_down):
    p_rows = hs.shape[0]
    d_ff = we_gate.shape[2]
    tm, tf = TM_FFN, TF_FFN
    n_f = d_ff // tf

    def row_map(i, f, te, nt):
        return (jnp.minimum(i, nt[0] - 1), 0)

    def w_col_map(i, f, te, nt):
        valid = i < nt[0]
        return (te[jnp.minimum(i, nt[0] - 1)], 0, jnp.where(valid, f, n_f - 1))

    def w_row_map(i, f, te, nt):
        valid = i < nt[0]
        return (te[jnp.minimum(i, nt[0] - 1)], jnp.where(valid, f, n_f - 1), 0)

    grid_spec = pltpu.PrefetchScalarGridSpec(
        num_scalar_prefetch=2,
        grid=(p_rows // tm, n_f),
        in_specs=[pl.BlockSpec((tm, D_MODEL), row_map),
                  pl.BlockSpec((None, D_MODEL, tf), w_col_map),
                  pl.BlockSpec((None, D_MODEL, tf), w_col_map),
                  pl.BlockSpec((None, tf, D_MODEL), w_row_map)],
        out_specs=pl.BlockSpec((tm, D_MODEL), lambda i, f, te, nt: (i, 0)),
    )
    return pl.pallas_call(
        _moe_ffn_kernel,
        grid_spec=grid_spec,
        out_shape=jax.ShapeDtypeStruct((p_rows, D_MODEL), F32),
        compiler_params=_params(("arbitrary", "arbitrary"), 56),
        name="moe_ffn",
    )(tile_expert, n_tiles, hs, we_gate, we_up, we_down)


def _router_kernel(h_ref, w_ref, meta_ref, count_ref, carry_ref):
    @pl.when(pl.program_id(0) == 0)
    def _():
        carry_ref[...] = jnp.zeros_like(carry_ref)

    tb = h_ref.shape[0]
    logits = jnp.dot(h_ref[...], w_ref[...], preferred_element_type=F32)
    lane = lax.broadcasted_iota(jnp.int32, logits.shape, 1)
    lg = jnp.where(lane < N_EXPERTS, logits, -jnp.inf)
    m1 = jnp.max(lg, axis=-1, keepdims=True)
    i1 = jnp.min(jnp.where(lg == m1, lane, LANES), axis=-1, keepdims=True)
    lg2 = jnp.where(lane == i1, -jnp.inf, lg)
    m2 = jnp.max(lg2, axis=-1, keepdims=True)
    i2 = jnp.min(jnp.where(lg2 == m2, lane, LANES), axis=-1, keepdims=True)
    e2 = jnp.exp(m2 - m1)
    w1 = 1.0 / (1.0 + e2)
    w2 = e2 / (1.0 + e2)
    sel1 = lane == i1
    sel2 = lane == i2
    assign = jnp.where(sel1, 1.0, jnp.where(sel2, 1.0, 0.0))
    r_io = lax.broadcasted_iota(jnp.int32, (tb, tb), 0)
    c_io = lax.broadcasted_iota(jnp.int32, (tb, tb), 1)
    lower = jnp.where(r_io > c_io, 1.0, 0.0).astype(BF16)
    ranks = jnp.dot(lower, assign.astype(BF16), preferred_element_type=F32) + carry_ref[...]
    r1 = jnp.sum(jnp.where(sel1, ranks, 0.0), axis=-1, keepdims=True)
    r2 = jnp.sum(jnp.where(sel2, ranks, 0.0), axis=-1, keepdims=True)
    new_carry = carry_ref[...] + jnp.sum(assign, axis=0, keepdims=True)
    carry_ref[...] = new_carry
    count_ref[...] = new_carry
    meta = jnp.where(lane == 0, i1.astype(F32),
           jnp.where(lane == 1, i2.astype(F32),
           jnp.where(lane == 2, w1,
           jnp.where(lane == 3, w2,
           jnp.where(lane == 4, r1,
           jnp.where(lane == 5, r2, 0.0))))))
    meta_ref[...] = meta


def _router(h, router_pad):
    n = h.shape[0]
    tb = TB_ROUTE
    return pl.pallas_call(
        _router_kernel,
        grid=(n // tb,),
        in_specs=[pl.BlockSpec((tb, D_MODEL), lambda i: (i, 0)),
                  pl.BlockSpec((D_MODEL, LANES), lambda i: (0, 0))],
        out_specs=[pl.BlockSpec((tb, LANES), lambda i: (i, 0)),
                   pl.BlockSpec((1, LANES), lambda i: (0, 0))],
        out_shape=[jax.ShapeDtypeStruct((n, LANES), F32),
                   jax.ShapeDtypeStruct((1, LANES), F32)],
        scratch_shapes=[pltpu.VMEM((1, LANES), F32)],
        compiler_params=_params(("arbitrary",), 32),
        name="router",
    )(h, router_pad)


def _dispatch_kernel(pad_start_ref, pad_len_ref, pos1_ref, pos2_ref, h_ref, hs_hbm,
                     zero_ref, sem, zsem):
    i = pl.program_id(0)
    tb = pos1_ref.shape[-1]

    @pl.when(i == 0)
    def _():
        zero_ref[...] = jnp.zeros_like(zero_ref)
        zb = zero_ref.shape[0]
        tail_start = pad_start_ref[N_EXPERTS]
        tail_blocks = pad_len_ref[N_EXPERTS] // zb

        def zero_block(r, _):
            pltpu.make_async_copy(zero_ref, hs_hbm.at[pl.ds(tail_start + r * zb, zb)], zsem).start()
            return 0

        def zero_block_wait(r, _):
            pltpu.make_async_copy(zero_ref, hs_hbm.at[pl.ds(0, zb)], zsem).wait()
            return 0

        def zero_row_wait(r, _):
            pltpu.make_async_copy(zero_ref.at[0], hs_hbm.at[0], zsem).wait()
            return 0

        lax.fori_loop(0, tail_blocks, zero_block, 0)
        for e in range(N_EXPERTS):
            def zero_row(r, _, e=e):
                pltpu.make_async_copy(zero_ref.at[0], hs_hbm.at[pad_start_ref[e] + r], zsem).start()
                return 0
            lax.fori_loop(0, pad_len_ref[e], zero_row, 0)
        lax.fori_loop(0, tail_blocks, zero_block_wait, 0)
        for e in range(N_EXPERTS):
            lax.fori_loop(0, pad_len_ref[e], zero_row_wait, 0)

    def move(t, _):
        pltpu.make_async_copy(h_ref.at[t], hs_hbm.at[pos1_ref[0, 0, t]], sem).start()
        pltpu.make_async_copy(h_ref.at[t], hs_hbm.at[pos2_ref[0, 0, t]], sem).start()
        return 0

    lax.fori_loop(0, tb, move, 0)
    for _ in range(2):
        pltpu.make_async_copy(h_ref, hs_hbm.at[pl.ds(0, tb)], sem).wait()


def _dispatch(pad_start, pad_len, pos1, pos2, h3, p_rows):
    n = h3.shape[0]
    tb = TB_DISPATCH
    sub = h3.shape[1]
    smem_row = pl.BlockSpec((1, 1, tb), lambda i, ps, pn: (i, 0, 0), memory_space=pltpu.SMEM)
    grid_spec = pltpu.PrefetchScalarGridSpec(
        num_scalar_prefetch=2,
        grid=(n // tb,),
        in_specs=[smem_row, smem_row,
                  pl.BlockSpec((tb, sub, LANES), lambda i, ps, pn: (i, 0, 0))],
        out_specs=pl.BlockSpec(memory_space=pl.ANY),
        scratch_shapes=[pltpu.VMEM((TB_DISPATCH, sub, LANES), BF16),
                        pltpu.SemaphoreType.DMA(()), pltpu.SemaphoreType.DMA(())],
    )
    return pl.pallas_call(
        _dispatch_kernel,
        grid_spec=grid_spec,
        out_shape=jax.ShapeDtypeStruct((p_rows, sub, LANES), BF16),
        compiler_params=_params(("arbitrary",), 16),
        name="moe_dispatch",
    )(pad_start, pad_len, pos1.reshape(n // tb, 1, tb), pos2.reshape(n // tb, 1, tb), h3)


def _combine_kernel(pos1_ref, pos2_ref, x_ref, meta_ref, o_hbm, out_ref, buf_ref, sem):
    tb = x_ref.shape[0]

    def fetch(t, _):
        pltpu.make_async_copy(o_hbm.at[pl.ds(pos1_ref[0, 0, t], 1)],
                              buf_ref.at[0, pl.ds(t, 1)], sem).start()
        pltpu.make_async_copy(o_hbm.at[pl.ds(pos2_ref[0, 0, t], 1)],
                              buf_ref.at[1, pl.ds(t, 1)], sem).start()
        return 0

    lax.fori_loop(0, tb, fetch, 0)
    for k in range(2):
        pltpu.make_async_copy(o_hbm.at[pl.ds(0, tb)], buf_ref.at[k], sem).wait()
    w1 = meta_ref[:, 2:3]
    w2 = meta_ref[:, 3:4]
    out_ref[...] = x_ref[...] + w1 * buf_ref[0] + w2 * buf_ref[1]


def _combine(pos1, pos2, x1, meta, o_sorted):
    n = x1.shape[0]
    tb = TB_COMBINE
    smem_row = pl.BlockSpec((1, 1, tb), lambda i: (i, 0, 0), memory_space=pltpu.SMEM)
    return pl.pallas_call(
        _combine_kernel,
        grid=(n // tb,),
        in_specs=[smem_row, smem_row,
                  pl.BlockSpec((tb, D_MODEL), lambda i: (i, 0)),
                  pl.BlockSpec((tb, LANES), lambda i: (i, 0)),
                  pl.BlockSpec(memory_space=pl.ANY)],
        out_specs=pl.BlockSpec((tb, D_MODEL), lambda i: (i, 0)),
        out_shape=jax.ShapeDtypeStruct((n, D_MODEL), F32),
        scratch_shapes=[pltpu.VMEM((2, tb, D_MODEL), F32), pltpu.SemaphoreType.DMA(())],
        compiler_params=_params(("arbitrary",), 32),
        name="moe_combine",
    )(pos1.reshape(n // tb, 1, tb), pos2.reshape(n // tb, 1, tb), x1, meta, o_sorted)


def _moe_layer(x1, h, router_w, we_gate, we_up, we_down):
    n = x1.shape[0]
    tm = TM_FFN
    p_rows = 2 * n + N_EXPERTS * tm
    router_pad = jnp.pad(router_w.astype(BF16), ((0, 0), (0, LANES - N_EXPERTS)))
    meta, counts = _router(h, router_pad)
    counts = counts[0, :N_EXPERTS].astype(jnp.int32)
    padded = ((counts + tm - 1) // tm) * tm
    ends = jnp.cumsum(padded)
    offsets = ends - padded
    e1 = meta[:, 0].astype(jnp.int32)
    e2 = meta[:, 1].astype(jnp.int32)
    pos1 = offsets[e1] + meta[:, 4].astype(jnp.int32)
    pos2 = offsets[e2] + meta[:, 5].astype(jnp.int32)
    n_tiles = (ends[-1] // tm).reshape(1)
    tile_ids = jnp.arange(p_rows // tm, dtype=jnp.int32)
    tile_expert = jnp.sum((tile_ids[:, None] >= (ends // tm)[None, :]).astype(jnp.int32), axis=1)
    tile_expert = jnp.minimum(tile_expert, N_EXPERTS - 1)
    pad_start = jnp.concatenate([offsets + counts, ends[-1:]])
    pad_len = jnp.concatenate([padded - counts, p_rows - ends[-1:]])

    h3 = h.reshape(n, D_MODEL // LANES, LANES)
    hs3 = _dispatch(pad_start, pad_len, pos1, pos2, h3, p_rows)
    hs = hs3.reshape(p_rows, D_MODEL)
    o_sorted = _moe_ffn(tile_expert, n_tiles, hs, we_gate, we_up, we_down)
    return _combine(pos1, pos2, x1, meta, o_sorted)


def _rope_tables(d):
    half = d // 4
    pos = jnp.arange(SEQ, dtype=jnp.int32)
    freqs = ROPE_THETA ** (-jnp.arange(half, dtype=F32) / half)

    def one(p):
        ang = p.astype(F32)[:, None] * freqs[None, :]
        c, s = jnp.cos(ang), jnp.sin(ang)
        return jnp.concatenate([c, c], -1), jnp.concatenate([-s, s], -1)

    cr, sr = one(pos // GRID_W)
    cc, sc = one(pos % GRID_W)
    return jnp.concatenate([cr, cc], -1), jnp.concatenate([sr, sc], -1)


def _pad_row(v, width):
    return jnp.pad(v.astype(F32), (0, width - v.shape[0]))


def kernel(x, mix_norm, w_in, rpb, qn_a, kn_a, qn_b, kn_b, cq_norm, ckv_norm, w_uq, w_ukv, qn_c, kn_c, on_a, on_b, on_c, w_out, ffn_norm, w_gate, w_up, w_down, router, we_gate, we_up, we_down):
    batch, seq, d = x.shape
    assert (seq, d) == (SEQ, D_MODEL)
    n = batch * seq
    depth = w_in.shape[0]
    x2 = x.reshape(n, d)

    cosb, sinb = _rope_tables(HEAD_DIM)
    cos64, sin64 = _rope_tables(C_ROPE)
    cosc = jnp.concatenate([cos64, cos64], -1)
    sinc = jnp.concatenate([sin64, sin64], -1)

    for l in range(depth):
        w_pad = jnp.pad(w_in[l].astype(BF16), ((0, 0), (0, D_IN_PAD - D_IN)))
        wuq = w_uq[l].astype(BF16).reshape(C_Q_RANK, C_HEADS, C_NOPE + C_ROPE)
        wuq = jnp.concatenate([wuq[..., :C_NOPE].reshape(C_Q_RANK, -1),
                               wuq[..., C_NOPE:].reshape(C_Q_RANK, -1)], -1)
        wukv = w_ukv[l].astype(BF16).reshape(C_KV_RANK, C_HEADS, C_NOPE + C_V)
        wukv = jnp.concatenate([wukv[..., :C_NOPE].reshape(C_KV_RANK, -1),
                                wukv[..., C_NOPE:].reshape(C_KV_RANK, -1)], -1)
        gains = jnp.stack([
            _pad_row(qn_a[l], 512), _pad_row(kn_a[l], 512),
            _pad_row(qn_b[l], 512), _pad_row(kn_b[l], 512),
            _pad_row(cq_norm[l], 512), _pad_row(ckv_norm[l], 512),
            _pad_row(qn_c[l][:C_NOPE], 512),
            _pad_row(jnp.concatenate([qn_c[l][C_NOPE:], qn_c[l][C_NOPE:]]), 512),
            _pad_row(kn_c[l][:C_NOPE], 512), _pad_row(kn_c[l][C_NOPE:], 512),
        ] + [jnp.zeros((512,), F32)] * 6)
        qa, ka, va, qb, kb, vb, qc, kc, vc = _in_proj(
            x2, mix_norm[l].reshape(1, d), w_pad, wuq, wukv, gains, cosb, sinb, cosc, sinc)

        oa = _nbr_attention(qa, ka, va, _nbr_bias_tables(rpb[l]), on_a[l].reshape(1, -1))
        ob = _flash_attention(qb, kb, vb, on_b[l].reshape(1, -1), n_heads=B_HEADS,
                              n_kv_heads=B_KV_HEADS, dq=HEAD_DIM, dv=HEAD_DIM,
                              scale=HEAD_DIM ** -0.5, tq=TQ_FLASH_B, units=2)
        oc = _flash_attention(qc, kc, vc, on_c[l].reshape(1, -1), n_heads=C_HEADS,
                              n_kv_heads=C_HEADS, dq=C_QK_PAD, dv=C_V,
                              scale=(C_NOPE + C_ROPE) ** -0.5, tq=TQ_FLASH_C, units=2)

        wo = w_out[l].astype(BF16)
        wa, wb, wc = wo[:A_WIDTH], wo[A_WIDTH:A_WIDTH + B_WIDTH], wo[A_WIDTH + B_WIDTH:]
        g_ffn = ffn_norm[l].reshape(1, d)
        if l % 2 == 0:
            (x1,) = _out_proj(x2, oa, ob, oc, wa, wb, wc, g_ffn, emit_h=False)
            x2 = _dense_ffn(x1, g_ffn, w_gate[l // 2], w_up[l // 2], w_down[l // 2])
        else:
            x1, h = _out_proj(x2, oa, ob, oc, wa, wb, wc, g_ffn, emit_h=True)
            x2 = _moe_layer(x1, h, router[l // 2], we_gate[l // 2], we_up[l // 2],
                            we_down[l // 2])
    return x2.reshape(batch, seq, d)
```

```python
import functools
import math

import jax
import jax.numpy as jnp
import numpy as np
from jax import lax
from jax.experimental import pallas as pl
from jax.experimental.pallas import tpu as pltpu

D_MODEL = 2048
SEQ = 4096
GRID_W = 64
GRID_ROWS = SEQ // GRID_W
HEAD_DIM = 128
A_HEADS = 4
WIN_H = 8
WIN_W = 16
B_HEADS = 8
B_KV_HEADS = 2
C_HEADS = 4
C_Q_RANK = 512
C_KV_RANK = 256
C_NOPE = 128
C_ROPE = 64
C_V = 128
A_WIDTH = A_HEADS * HEAD_DIM
B_WIDTH = B_HEADS * HEAD_DIM
B_KV_WIDTH = B_KV_HEADS * HEAD_DIM
C_WIDTH = C_HEADS * C_V
C_QK_PAD = 256
D_IN = 3 * A_WIDTH + B_WIDTH + 2 * B_KV_WIDTH + C_Q_RANK + C_KV_RANK + C_ROPE
D_IN_PAD = D_IN + 64
N_EXPERTS = 8
ROPE_THETA = 10000.0
EPS = 1e-6
NEG_BIG = -1e30

LANES = 128
MIB = 1024 * 1024

BF16 = jnp.bfloat16
F32 = jnp.float32

TM_IN = 512
TQ_NBR = 4 * GRID_W
TQ_FLASH_B = 256
TQ_FLASH_C = 512
TK_FLASH = 1024
TM_OUT = 512
TM_FFN = 1024
TF_FFN = 256
TB_ROUTE = 512
TB_DISPATCH = 256
TB_COMBINE = 256

_TRANS_B = (((1,), (1,)), ((), ()))


def _params(semantics, vmem_mib):
    return pltpu.CompilerParams(dimension_semantics=semantics,
                                vmem_limit_bytes=vmem_mib * MIB)


def _rms(y, gain, n):
    ms = jnp.sum(y * y, axis=-1, keepdims=True) * (1.0 / n)
    return y * lax.rsqrt(ms + EPS) * gain


def _rope(y, cos_t, sin_t, half):
    lane = lax.broadcasted_iota(jnp.int32, y.shape, 1)
    first = (lane & half) == 0
    swapped = jnp.where(first, pltpu.roll(y, LANES - half, 1), pltpu.roll(y, half, 1))
    return y * cos_t + swapped * sin_t


def _in_proj_kernel(x_ref, gmix_ref, w_ref, wuq_ref, wukv_ref, gains_ref,
                    cosb_ref, sinb_ref, cosc_ref, sinc_ref,
                    qa_ref, ka_ref, va_ref, qb_ref, kb_ref, vb_ref,
                    qc_ref, kc_ref, vc_ref):
    x = x_ref[...]
    xn = _rms(x, gmix_ref[...], D_MODEL).astype(BF16)

    def proj(c0, c1):
        return jnp.dot(xn, w_ref[:, c0:c1], preferred_element_type=F32)

    def gain(row, width=LANES):
        return gains_ref[row:row + 1, 0:width]

    cosb, sinb = cosb_ref[...], sinb_ref[...]
    cosc, sinc = cosc_ref[...], sinc_ref[...]

    col = 0
    y = proj(col, col + A_WIDTH)
    for h in range(A_HEADS):
        sl = slice(h * HEAD_DIM, (h + 1) * HEAD_DIM)
        qa_ref[:, sl] = _rms(y[:, sl], gain(0), HEAD_DIM).astype(BF16)
    col += A_WIDTH
    y = proj(col, col + A_WIDTH)
    for h in range(A_HEADS):
        sl = slice(h * HEAD_DIM, (h + 1) * HEAD_DIM)
        ka_ref[:, sl] = _rms(y[:, sl], gain(1), HEAD_DIM).astype(BF16)
    col += A_WIDTH
    va_ref[...] = proj(col, col + A_WIDTH).astype(BF16)
    col += A_WIDTH

    for c in range(B_WIDTH // 512):
        y = proj(col + 512 * c, col + 512 * (c + 1))
        for hh in range(4):
            sl = slice(hh * HEAD_DIM, (hh + 1) * HEAD_DIM)
            out = _rope(_rms(y[:, sl], gain(2), HEAD_DIM), cosb, sinb, 32)
            h = 4 * c + hh
            qb_ref[:, h * HEAD_DIM:(h + 1) * HEAD_DIM] = out.astype(BF16)
    col += B_WIDTH
    y = proj(col, col + 2 * B_KV_WIDTH)
    for h in range(B_KV_HEADS):
        sl = slice(h * HEAD_DIM, (h + 1) * HEAD_DIM)
        kb_ref[:, sl] = _rope(_rms(y[:, sl], gain(3), HEAD_DIM), cosb, sinb, 32).astype(BF16)
    vb_ref[...] = y[:, B_KV_WIDTH:].astype(BF16)
    col += 2 * B_KV_WIDTH

    lane = lax.broadcasted_iota(jnp.int32, (x.shape[0], LANES), 1)
    low = lane < C_ROPE
    cqn = _rms(proj(col, col + C_Q_RANK), gain(4, C_Q_RANK), C_Q_RANK).astype(BF16)
    col += C_Q_RANK
    yq = jnp.dot(cqn, wuq_ref[...], preferred_element_type=F32)
    for p in range(C_HEADS // 2):
        pe2 = yq[:, C_HEADS * C_NOPE + p * LANES:C_HEADS * C_NOPE + (p + 1) * LANES]
        sq = pe2 * pe2
        ms_lo = jnp.sum(jnp.where(low, sq, 0.0), axis=-1, keepdims=True) * (1.0 / C_ROPE)
        ms_hi = jnp.sum(jnp.where(low, 0.0, sq), axis=-1, keepdims=True) * (1.0 / C_ROPE)
        r = jnp.where(low, lax.rsqrt(ms_lo + EPS), lax.rsqrt(ms_hi + EPS))
        roped = _rope(pe2 * r * gain(7), cosc, sinc, 16)
        for hh in range(2):
            h = 2 * p + hh
            nope = _rms(yq[:, h * C_NOPE:(h + 1) * C_NOPE], gain(6), C_NOPE)
            pe = roped if hh == 0 else pltpu.roll(roped, C_ROPE, 1)
            base = h * C_QK_PAD
            qc_ref[:, base:base + C_NOPE] = nope.astype(BF16)
            qc_ref[:, base + C_NOPE:base + C_QK_PAD] = jnp.where(low, pe, 0.0).astype(BF16)

    ckvn = _rms(proj(col, col + C_KV_RANK), gain(5, C_KV_RANK), C_KV_RANK).astype(BF16)
    col += C_KV_RANK
    ykv = jnp.dot(ckvn, wukv_ref[...], preferred_element_type=F32)
    ype = proj(col, col + LANES)
    ms = jnp.sum(ype * ype, axis=-1, keepdims=True) * (1.0 / C_ROPE)
    kpe = _rope(ype * lax.rsqrt(ms + EPS) * gain(9), cosc, sinc, 16)
    kpe = jnp.where(low, kpe, 0.0).astype(BF16)
    for h in range(C_HEADS):
        base = h * C_QK_PAD
        k_nope = _rms(ykv[:, h * C_NOPE:(h + 1) * C_NOPE], gain(8), C_NOPE)
        kc_ref[:, base:base + C_NOPE] = k_nope.astype(BF16)
        kc_ref[:, base + C_NOPE:base + C_QK_PAD] = kpe
    vc_ref[...] = ykv[:, C_HEADS * C_NOPE:].astype(BF16)


def _in_proj(x2, gmix, w_pad, wuq, wukv, gains, cosb, sinb, cosc, sinc):
    n = x2.shape[0]
    tm = TM_IN
    seq_blocks = SEQ // tm
    row = lambda i: (i, 0)
    const = lambda i: (0, 0)
    pos = lambda i: (i % seq_blocks, 0)
    resident = pl.Buffered(1)
    widths = (A_WIDTH, A_WIDTH, A_WIDTH, B_WIDTH, B_KV_WIDTH, B_KV_WIDTH,
              C_HEADS * C_QK_PAD, C_HEADS * C_QK_PAD, C_WIDTH)
    return pl.pallas_call(
        _in_proj_kernel,
        grid=(n // tm,),
        in_specs=[
            pl.BlockSpec((tm, D_MODEL), row),
            pl.BlockSpec((1, D_MODEL), const),
            pl.BlockSpec((D_MODEL, D_IN_PAD), const, pipeline_mode=resident),
            pl.BlockSpec(wuq.shape, const, pipeline_mode=resident),
            pl.BlockSpec(wukv.shape, const, pipeline_mode=resident),
            pl.BlockSpec(gains.shape, const),
            pl.BlockSpec((tm, LANES), pos), pl.BlockSpec((tm, LANES), pos),
            pl.BlockSpec((tm, LANES), pos), pl.BlockSpec((tm, LANES), pos),
        ],
        out_specs=[pl.BlockSpec((tm, w), row) for w in widths],
        out_shape=[jax.ShapeDtypeStruct((n, w), BF16) for w in widths],
        compiler_params=_params(("parallel",), 56),
        name="in_proj",
    )(x2, gmix, w_pad, wuq, wukv, gains, cosb, sinb, cosc, sinc)


def _nbr_kernel(q_ref, k0_ref, k1_ref, k2_ref, v0_ref, v1_ref, v2_ref,
                bias_ref, gain_ref, o_ref):
    scale = HEAD_DIM ** -0.5
    k_refs = (k0_ref, k1_ref, k2_ref)
    v_refs = (v0_ref, v1_ref, v2_ref)
    outs = []
    for h in range(A_HEADS):
        sl = slice(h * HEAD_DIM, (h + 1) * HEAD_DIM)
        q = q_ref[:, sl]
        s = []
        for j in range(3):
            sj = lax.dot_general(q, k_refs[j][:, sl], _TRANS_B, preferred_element_type=F32)
            s.append(sj * scale + bias_ref[h, :, j * TQ_NBR:(j + 1) * TQ_NBR])
        m = jnp.maximum(jnp.maximum(jnp.max(s[0], axis=-1, keepdims=True),
                                    jnp.max(s[1], axis=-1, keepdims=True)),
                        jnp.max(s[2], axis=-1, keepdims=True))
        p = [jnp.exp(sj - m) for sj in s]
        l = (jnp.sum(p[0], axis=-1, keepdims=True) + jnp.sum(p[1], axis=-1, keepdims=True)
             + jnp.sum(p[2], axis=-1, keepdims=True))
        o = jnp.dot(p[0].astype(BF16), v_refs[0][:, sl], preferred_element_type=F32)
        o += jnp.dot(p[1].astype(BF16), v_refs[1][:, sl], preferred_element_type=F32)
        o += jnp.dot(p[2].astype(BF16), v_refs[2][:, sl], preferred_element_type=F32)
        outs.append(o / l)
    o_all = jnp.concatenate(outs, axis=-1)
    o_ref[...] = _rms(o_all, gain_ref[...], A_WIDTH).astype(BF16)


def _nbr_bias_tables(rpb_l):
    heads, n_dr, n_dc = rpb_l.shape
    blocks = GRID_ROWS // 4
    shift = GRID_W - WIN_W
    padded = jnp.pad(rpb_l.astype(F32), ((0, 0), (0, 0), (shift, LANES - shift - n_dc)))
    toep = jnp.broadcast_to(padded[:, :, None, :], (heads, n_dr, GRID_W, LANES))
    toep = toep.reshape(heads, n_dr, GRID_W * LANES)[:, :, :GRID_W * (LANES - 1)]
    toep = toep.reshape(heads, n_dr, GRID_W, LANES - 1)[:, :, :, GRID_W - 1:]
    masked = jnp.full((heads, GRID_W, GRID_W), NEG_BIG, F32)
    classes = []
    for u in (0, 1, blocks - 1):
        start = 4 * min(max(u - 1, 0), blocks - 3)
        per_query_row = []
        for a in range(4):
            qr = 4 * u + a
            r0 = min(max(qr - WIN_H // 2, 0), GRID_ROWS - WIN_H)
            rows = [toep[:, start + i - qr + WIN_H - 1] if r0 <= start + i < r0 + WIN_H else masked
                    for i in range(12)]
            per_query_row.append(jnp.stack(rows, axis=2))
        classes.append(jnp.stack(per_query_row, axis=1))
    table = jnp.stack(classes, axis=1).reshape(heads, 3, TQ_NBR, 3 * TQ_NBR)
    cols = np.arange(GRID_W)
    c0 = np.clip(cols - WIN_W // 2, 0, GRID_W - WIN_W)
    ok_c = (cols[None, :] >= c0[:, None]) & (cols[None, :] < c0[:, None] + WIN_W)
    return jnp.where(np.tile(ok_c, (4, 12)), table, NEG_BIG)


def _nbr_attention(qa, ka, va, bias, gain):
    n = qa.shape[0]
    batch = n // SEQ
    blocks = SEQ // TQ_NBR

    def q_map(b, u):
        return (b * blocks + u, 0)

    def kv_map(j):
        return lambda b, u: (b * blocks + jnp.clip(u - 1, 0, blocks - 3) + j, 0)

    def bias_map(b, u):
        cls = jnp.where(u == 0, 0, jnp.where(u == blocks - 1, 2, 1))
        return (0, cls, 0, 0)

    tile = lambda m: pl.BlockSpec((TQ_NBR, A_WIDTH), m)
    return pl.pallas_call(
        _nbr_kernel,
        grid=(batch, blocks),
        in_specs=[tile(q_map), tile(kv_map(0)), tile(kv_map(1)), tile(kv_map(2)),
                  tile(kv_map(0)), tile(kv_map(1)), tile(kv_map(2)),
                  pl.BlockSpec((A_HEADS, None, TQ_NBR, 3 * TQ_NBR), bias_map),
                  pl.BlockSpec((1, A_WIDTH), lambda b, u: (0, 0))],
        out_specs=tile(q_map),
        out_shape=jax.ShapeDtypeStruct((n, A_WIDTH), BF16),
        compiler_params=_params(("parallel", "arbitrary"), 40),
        name="nbr_attention",
    )(qa, ka, ka, ka, va, va, va, bias, gain)


def _flash_kernel(q_ref, k_ref, v_ref, gain_ref, o_ref,
                  qs_ref, m_ref, l_ref, acc_ref, out_ref, *,
                  n_kv_heads, group, units, dq, dv, tk, exp2_scale):
    tq = q_ref.shape[0]
    n_kv = k_ref.shape[0] // tk
    tiles = tk // LANES
    for first in range(0, n_kv_heads, units):
        for u in range(units):
            for hh in range(group):
                h = (first + u) * group + hh
                qs_ref[u, hh * tq:(hh + 1) * tq, :] = q_ref[:, h * dq:(h + 1) * dq]
        m_ref[...] = jnp.full(m_ref.shape, -jnp.inf, F32)
        l_ref[...] = jnp.zeros(l_ref.shape, F32)
        acc_ref[...] = jnp.zeros(acc_ref.shape, F32)

        def body(j, _, first=first):
            off = pl.multiple_of(j * tk, tk)
            for u in range(units):
                g = first + u
                k = k_ref[pl.ds(off, tk), g * dq:(g + 1) * dq]
                v = v_ref[pl.ds(off, tk), g * dv:(g + 1) * dv]
                s = lax.dot_general(qs_ref[u], k, _TRANS_B, preferred_element_type=F32)
                mx = s[:, 0:LANES]
                for t in range(1, tiles):
                    mx = jnp.maximum(mx, s[:, t * LANES:(t + 1) * LANES])
                m_old = m_ref[u]
                m_new = jnp.maximum(m_old, jnp.max(mx, axis=-1, keepdims=True))
                alpha = jnp.exp2((m_old - m_new) * exp2_scale)
                lsum = alpha * l_ref[u]
                p = []
                for t in range(tiles):
                    pt = jnp.exp2((s[:, t * LANES:(t + 1) * LANES] - m_new) * exp2_scale)
                    lsum = lsum + pt
                    p.append(pt.astype(BF16))
                pv = jnp.dot(jnp.concatenate(p, axis=1), v, preferred_element_type=F32)
                acc_ref[u] = alpha * acc_ref[u] + pv
                m_ref[u] = m_new
                l_ref[u] = lsum
            return 0

        lax.fori_loop(0, n_kv, body, 0)
        for u in range(units):
            o = acc_ref[u] / jnp.sum(l_ref[u], axis=-1, keepdims=True)
            for hh in range(group):
                h = (first + u) * group + hh
                out_ref[:, h * dv:(h + 1) * dv] = o[hh * tq:(hh + 1) * tq]
    width = n_kv_heads * group * dv
    o_ref[...] = _rms(out_ref[...], gain_ref[...], width).astype(BF16)


def _flash_attention(q, k, v, gain, *, n_heads, n_kv_heads, dq, dv, scale, tq, units):
    assert dv == LANES and n_kv_heads % units == 0
    n = q.shape[0]
    batch = n // SEQ
    q_blocks = SEQ // tq
    width = n_heads * dv
    group = n_heads // n_kv_heads
    rows = group * tq
    kernel = functools.partial(
        _flash_kernel, n_kv_heads=n_kv_heads, group=group, units=units, dq=dq, dv=dv,
        tk=TK_FLASH, exp2_scale=scale * math.log2(math.e))
    return pl.pallas_call(
        kernel,
        grid=(batch, q_blocks),
        in_specs=[pl.BlockSpec((tq, n_heads * dq), lambda b, i: (b * q_blocks + i, 0)),
                  pl.BlockSpec((SEQ, n_kv_heads * dq), lambda b, i: (b, 0)),
                  pl.BlockSpec((SEQ, n_kv_heads * dv), lambda b, i: (b, 0)),
                  pl.BlockSpec((1, width), lambda b, i: (0, 0))],
        out_specs=pl.BlockSpec((tq, width), lambda b, i: (b * q_blocks + i, 0)),
        out_shape=jax.ShapeDtypeStruct((n, width), BF16),
        scratch_shapes=[pltpu.VMEM((units, rows, dq), BF16),
                        pltpu.VMEM((units, rows, LANES), F32),
                        pltpu.VMEM((units, rows, LANES), F32),
                        pltpu.VMEM((units, rows, dv), F32),
                        pltpu.VMEM((tq, width), F32)],
        compiler_params=_params(("parallel", "arbitrary"), 48),
        name=f"flash_attention_h{n_heads}",
    )(q, k, v, gain)


def _out_proj_kernel(x_ref, oa_ref, ob_ref, oc_ref, wa_ref, wb_ref, wc_ref, g_ref,
                     x1_ref, *h_ref):
    acc = x_ref[...] + jnp.dot(oa_ref[...], wa_ref[...], preferred_element_type=F32)
    acc += jnp.dot(ob_ref[...], wb_ref[...], preferred_element_type=F32)
    acc += jnp.dot(oc_ref[...], wc_ref[...], preferred_element_type=F32)
    x1_ref[...] = acc
    if h_ref:
        h_ref[0][...] = _rms(acc, g_ref[...], D_MODEL).astype(BF16)


def _out_proj(x2, oa, ob, oc, wa, wb, wc, g_ffn, emit_h):
    n = x2.shape[0]
    tm = TM_OUT
    row = lambda i: (i, 0)
    const = lambda i: (0, 0)
    resident = pl.Buffered(1)
    out_specs = [pl.BlockSpec((tm, D_MODEL), row)]
    out_shape = [jax.ShapeDtypeStruct((n, D_MODEL), F32)]
    if emit_h:
        out_specs.append(pl.BlockSpec((tm, D_MODEL), row))
        out_shape.append(jax.ShapeDtypeStruct((n, D_MODEL), BF16))
    return pl.pallas_call(
        _out_proj_kernel,
        grid=(n // tm,),
        in_specs=[pl.BlockSpec((tm, D_MODEL), row),
                  pl.BlockSpec((tm, A_WIDTH), row),
                  pl.BlockSpec((tm, B_WIDTH), row),
                  pl.BlockSpec((tm, C_WIDTH), row),
                  pl.BlockSpec(wa.shape, const, pipeline_mode=resident),
                  pl.BlockSpec(wb.shape, const, pipeline_mode=resident),
                  pl.BlockSpec(wc.shape, const, pipeline_mode=resident),
                  pl.BlockSpec((1, D_MODEL), const)],
        out_specs=out_specs,
        out_shape=out_shape,
        compiler_params=_params(("parallel",), 48),
        name="out_proj",
    )(x2, oa, ob, oc, wa, wb, wc, g_ffn)


def _swiglu_step(h, wg_ref, wu_ref, wd_ref):
    gate = jnp.dot(h, wg_ref[...].astype(BF16), preferred_element_type=F32)
    up = jnp.dot(h, wu_ref[...].astype(BF16), preferred_element_type=F32)
    act = (gate * jax.nn.sigmoid(gate) * up).astype(BF16)
    return jnp.dot(act, wd_ref[...].astype(BF16), preferred_element_type=F32)


def _dense_ffn_kernel(x_ref, g_ref, wg_ref, wu_ref, wd_ref, o_ref, h_ref):
    @pl.when(pl.program_id(1) == 0)
    def _():
        x = x_ref[...]
        h_ref[...] = _rms(x, g_ref[...], D_MODEL).astype(BF16)
        o_ref[...] = x

    o_ref[...] += _swiglu_step(h_ref[...], wg_ref, wu_ref, wd_ref)


def _dense_ffn(x2, g, w_gate, w_up, w_down):
    n = x2.shape[0]
    d_ff = w_gate.shape[1]
    tm, tf = TM_FFN, TF_FFN
    return pl.pallas_call(
        _dense_ffn_kernel,
        grid=(n // tm, d_ff // tf),
        in_specs=[pl.BlockSpec((tm, D_MODEL), lambda i, f: (i, 0), pipeline_mode=pl.Buffered(1)),
                  pl.BlockSpec((1, D_MODEL), lambda i, f: (0, 0)),
                  pl.BlockSpec((D_MODEL, tf), lambda i, f: (0, f)),
                  pl.BlockSpec((D_MODEL, tf), lambda i, f: (0, f)),
                  pl.BlockSpec((tf, D_MODEL), lambda i, f: (f, 0))],
        out_specs=pl.BlockSpec((tm, D_MODEL), lambda i, f: (i, 0)),
        out_shape=jax.ShapeDtypeStruct((n, D_MODEL), F32),
        scratch_shapes=[pltpu.VMEM((tm, D_MODEL), BF16)],
        compiler_params=_params(("parallel", "arbitrary"), 56),
        name="dense_ffn",
    )(x2, g, w_gate, w_up, w_down)


def _moe_ffn_kernel(tile_expert_ref, n_tiles_ref, h_ref, wg_ref, wu_ref, wd_ref, o_ref):
    i, f = pl.program_id(0), pl.program_id(1)

    @pl.when(f == 0)
    def _():
        o_ref[...] = jnp.zeros_like(o_ref)

    @pl.when(i < n_tiles_ref[0])
    def _():
        o_ref[...] += _swiglu_step(h_ref[...], wg_ref, wu_ref, wd_ref)


def _moe_ffn(tile_expert, n_tiles, hs, we_gate, we_up, we_down):
    p_rows = hs.shape[0]
    d_ff = we_gate.shape[2]
    tm, tf = TM_FFN, TF_FFN
    n_f = d_ff // tf

    def row_map(i, f, te, nt):
        return (jnp.minimum(i, nt[0] - 1), 0)

    def w_col_map(i, f, te, nt):
        valid = i < nt[0]
        return (te[jnp.minimum(i, nt[0] - 1)], 0, jnp.where(valid, f, n_f - 1))

    def w_row_map(i, f, te, nt):
        valid = i < nt[0]
        return (te[jnp.minimum(i, nt[0] - 1)], jnp.where(valid, f, n_f - 1), 0)

    grid_spec = pltpu.PrefetchScalarGridSpec(
        num_scalar_prefetch=2,
        grid=(p_rows // tm, n_f),
        in_specs=[pl.BlockSpec((tm, D_MODEL), row_map),
                  pl.BlockSpec((None, D_MODEL, tf), w_col_map),
                  pl.BlockSpec((None, D_MODEL, tf), w_col_map),
                  pl.BlockSpec((None, tf, D_MODEL), w_row_map)],
        out_specs=pl.BlockSpec((tm, D_MODEL), lambda i, f, te, nt: (i, 0)),
    )
    return pl.pallas_call(
        _moe_ffn_kernel,
        grid_spec=grid_spec,
        out_shape=jax.ShapeDtypeStruct((p_rows, D_MODEL), F32),
        compiler_params=_params(("arbitrary", "arbitrary"), 56),
        name="moe_ffn",
    )(tile_expert, n_tiles, hs, we_gate, we_up, we_down)


def _router_kernel(h_ref, w_ref, meta_ref, count_ref, carry_ref):
    @pl.when(pl.program_id(0) == 0)
    def _():
        carry_ref[...] = jnp.zeros_like(carry_ref)

    tb = h_ref.shape[0]
    logits = jnp.dot(h_ref[...], w_ref[...], preferred_element_type=F32)
    lane = lax.broadcasted_iota(jnp.int32, logits.shape, 1)
    lg = jnp.where(lane < N_EXPERTS, logits, -jnp.inf)
    m1 = jnp.max(lg, axis=-1, keepdims=True)
    i1 = jnp.min(jnp.where(lg == m1, lane, LANES), axis=-1, keepdims=True)
    lg2 = jnp.where(lane == i1, -jnp.inf, lg)
    m2 = jnp.max(lg2, axis=-1, keepdims=True)
    i2 = jnp.min(jnp.where(lg2 == m2, lane, LANES), axis=-1, keepdims=True)
    e2 = jnp.exp(m2 - m1)
    w1 = 1.0 / (1.0 + e2)
    w2 = e2 / (1.0 + e2)
    sel1 = lane == i1
    sel2 = lane == i2
    assign = jnp.where(sel1, 1.0, jnp.where(sel2, 1.0, 0.0))
    r_io = lax.broadcasted_iota(jnp.int32, (tb, tb), 0)
    c_io = lax.broadcasted_iota(jnp.int32, (tb, tb), 1)
    lower = jnp.where(r_io > c_io, 1.0, 0.0).astype(BF16)
    ranks = jnp.dot(lower, assign.astype(BF16), preferred_element_type=F32) + carry_ref[...]
    r1 = jnp.sum(jnp.where(sel1, ranks, 0.0), axis=-1, keepdims=True)
    r2 = jnp.sum(jnp.where(sel2, ranks, 0.0), axis=-1, keepdims=True)
    new_carry = carry_ref[...] + jnp.sum(assign, axis=0, keepdims=True)
    carry_ref[...] = new_carry
    count_ref[...] = new_carry
    meta = jnp.where(lane == 0, i1.astype(F32),
           jnp.where(lane == 1, i2.astype(F32),
           jnp.where(lane == 2, w1,
           jnp.where(lane == 3, w2,
           jnp.where(lane == 4, r1,
           jnp.where(lane == 5, r2, 0.0))))))
    meta_ref[...] = meta


def _router(h, router_pad):
    n = h.shape[0]
    tb = TB_ROUTE
    return pl.pallas_call(
        _router_kernel,
        grid=(n // tb,),
        in_specs=[pl.BlockSpec((tb, D_MODEL), lambda i: (i, 0)),
                  pl.BlockSpec((D_MODEL, LANES), lambda i: (0, 0))],
        out_specs=[pl.BlockSpec((tb, LANES), lambda i: (i, 0)),
                   pl.BlockSpec((1, LANES), lambda i: (0, 0))],
        out_shape=[jax.ShapeDtypeStruct((n, LANES), F32),
                   jax.ShapeDtypeStruct((1, LANES), F32)],
        scratch_shapes=[pltpu.VMEM((1, LANES), F32)],
        compiler_params=_params(("arbitrary",), 32),
        name="router",
    )(h, router_pad)


def _dispatch_kernel(pad_start_ref, pad_len_ref, pos1_ref, pos2_ref, h_ref, hs_hbm,
                     zero_ref, sem, zsem):
    i = pl.program_id(0)
    tb = pos1_ref.shape[-1]

    @pl.when(i == 0)
    def _():
        zero_ref[...] = jnp.zeros_like(zero_ref)
        zb = zero_ref.shape[0]
        tail_start = pad_start_ref[N_EXPERTS]
        tail_blocks = pad_len_ref[N_EXPERTS] // zb

        def zero_block(r, _):
            pltpu.make_async_copy(zero_ref, hs_hbm.at[pl.ds(tail_start + r * zb, zb)], zsem).start()
            return 0

        def zero_block_wait(r, _):
            pltpu.make_async_copy(zero_ref, hs_hbm.at[pl.ds(0, zb)], zsem).wait()
            return 0

        def zero_row_wait(r, _):
            pltpu.make_async_copy(zero_ref.at[0], hs_hbm.at[0], zsem).wait()
            return 0

        lax.fori_loop(0, tail_blocks, zero_block, 0)
        for e in range(N_EXPERTS):
            def zero_row(r, _, e=e):
                pltpu.make_async_copy(zero_ref.at[0], hs_hbm.at[pad_start_ref[e] + r], zsem).start()
                return 0
            lax.fori_loop(0, pad_len_ref[e], zero_row, 0)
        lax.fori_loop(0, tail_blocks, zero_block_wait, 0)
        for e in range(N_EXPERTS):
            lax.fori_loop(0, pad_len_ref[e], zero_row_wait, 0)

    def move(t, _):
        pltpu.make_async_copy(h_ref.at[t], hs_hbm.at[pos1_ref[0, 0, t]], sem).start()
        pltpu.make_async_copy(h_ref.at[t], hs_hbm.at[pos2_ref[0, 0, t]], sem).start(priority=1)
        return 0

    lax.fori_loop(0, tb, move, 0)
    for _ in range(2):
        pltpu.make_async_copy(h_ref, hs_hbm.at[pl.ds(0, tb)], sem).wait()


def _dispatch(pad_start, pad_len, pos1, pos2, h3, p_rows):
    n = h3.shape[0]
    tb = TB_DISPATCH
    sub = h3.shape[1]
    smem_row = pl.BlockSpec((1, 1, tb), lambda i, ps, pn: (i, 0, 0), memory_space=pltpu.SMEM)
    grid_spec = pltpu.PrefetchScalarGridSpec(
        num_scalar_prefetch=2,
        grid=(n // tb,),
        in_specs=[smem_row, smem_row,
                  pl.BlockSpec((tb, sub, LANES), lambda i, ps, pn: (i, 0, 0))],
        out_specs=pl.BlockSpec(memory_space=pl.ANY),
        scratch_shapes=[pltpu.VMEM((TB_DISPATCH, sub, LANES), BF16),
                        pltpu.SemaphoreType.DMA(()), pltpu.SemaphoreType.DMA(())],
    )
    return pl.pallas_call(
        _dispatch_kernel,
        grid_spec=grid_spec,
        out_shape=jax.ShapeDtypeStruct((p_rows, sub, LANES), BF16),
        compiler_params=_params(("arbitrary",), 16),
        name="moe_dispatch",
    )(pad_start, pad_len, pos1.reshape(n // tb, 1, tb), pos2.reshape(n // tb, 1, tb), h3)


def _combine_kernel(pos1_ref, pos2_ref, x_ref, meta_ref, o_hbm, out_ref, buf_ref, sem):
    tb = x_ref.shape[0]

    def fetch(t, _):
        pltpu.make_async_copy(o_hbm.at[pl.ds(pos1_ref[0, 0, t], 1)],
                              buf_ref.at[0, pl.ds(t, 1)], sem).start()
        pltpu.make_async_copy(o_hbm.at[pl.ds(pos2_ref[0, 0, t], 1)],
                              buf_ref.at[1, pl.ds(t, 1)], sem).start(priority=1)
        return 0

    lax.fori_loop(0, tb, fetch, 0)
    for k in range(2):
        pltpu.make_async_copy(o_hbm.at[pl.ds(0, tb)], buf_ref.at[k], sem).wait()
    w1 = meta_ref[:, 2:3]
    w2 = meta_ref[:, 3:4]
    out_ref[...] = x_ref[...] + w1 * buf_ref[0] + w2 * buf_ref[1]


def _combine(pos1, pos2, x1, meta, o_sorted):
    n = x1.shape[0]
    tb = TB_COMBINE
    smem_row = pl.BlockSpec((1, 1, tb), lambda i: (i, 0, 0), memory_space=pltpu.SMEM)
    return pl.pallas_call(
        _combine_kernel,
        grid=(n // tb,),
        in_specs=[smem_row, smem_row,
                  pl.BlockSpec((tb, D_MODEL), lambda i: (i, 0)),
                  pl.BlockSpec((tb, LANES), lambda i: (i, 0)),
                  pl.BlockSpec(memory_space=pl.ANY)],
        out_specs=pl.BlockSpec((tb, D_MODEL), lambda i: (i, 0)),
        out_shape=jax.ShapeDtypeStruct((n, D_MODEL), F32),
        scratch_shapes=[pltpu.VMEM((2, tb, D_MODEL), F32), pltpu.SemaphoreType.DMA(())],
        compiler_params=_params(("arbitrary",), 32),
        name="moe_combine",
    )(pos1.reshape(n // tb, 1, tb), pos2.reshape(n // tb, 1, tb), x1, meta, o_sorted)


def _moe_layer(x1, h, router_w, we_gate, we_up, we_down):
    n = x1.shape[0]
    tm = TM_FFN
    p_rows = 2 * n + N_EXPERTS * tm
    router_pad = jnp.pad(router_w.astype(BF16), ((0, 0), (0, LANES - N_EXPERTS)))
    meta, counts = _router(h, router_pad)
    counts = counts[0, :N_EXPERTS].astype(jnp.int32)
    padded = ((counts + tm - 1) // tm) * tm
    ends = jnp.cumsum(padded)
    offsets = ends - padded
    e1 = meta[:, 0].astype(jnp.int32)
    e2 = meta[:, 1].astype(jnp.int32)
    pos1 = offsets[e1] + meta[:, 4].astype(jnp.int32)
    pos2 = offsets[e2] + meta[:, 5].astype(jnp.int32)
    n_tiles = (ends[-1] // tm).reshape(1)
    tile_ids = jnp.arange(p_rows // tm, dtype=jnp.int32)
    tile_expert = jnp.sum((tile_ids[:, None] >= (ends // tm)[None, :]).astype(jnp.int32), axis=1)
    tile_expert = jnp.minimum(tile_expert, N_EXPERTS - 1)
    pad_start = jnp.concatenate([offsets + counts, ends[-1:]])
    pad_len = jnp.concatenate([padded - counts, p_rows - ends[-1:]])

    h3 = h.reshape(n, D_MODEL // LANES, LANES)
    hs3 = _dispatch(pad_start, pad_len, pos1, pos2, h3, p_rows)
    hs = hs3.reshape(p_rows, D_MODEL)
    o_sorted = _moe_ffn(tile_expert, n_tiles, hs, we_gate, we_up, we_down)
    return _combine(pos1, pos2, x1, meta, o_sorted)


def _rope_tables(d):
    half = d // 4
    pos = jnp.arange(SEQ, dtype=jnp.int32)
    freqs = ROPE_THETA ** (-jnp.arange(half, dtype=F32) / half)

    def one(p):
        ang = p.astype(F32)[:, None] * freqs[None, :]
        c, s = jnp.cos(ang), jnp.sin(ang)
        return jnp.concatenate([c, c], -1), jnp.concatenate([-s, s], -1)

    cr, sr = one(pos // GRID_W)
    cc, sc = one(pos % GRID_W)
    return jnp.concatenate([cr, cc], -1), jnp.concatenate([sr, sc], -1)


def _pad_row(v, width):
    return jnp.pad(v.astype(F32), (0, width - v.shape[0]))


def kernel(x, mix_norm, w_in, rpb, qn_a, kn_a, qn_b, kn_b, cq_norm, ckv_norm, w_uq, w_ukv, qn_c, kn_c, on_a, on_b, on_c, w_out, ffn_norm, w_gate, w_up, w_down, router, we_gate, we_up, we_down):
    batch, seq, d = x.shape
    assert (seq, d) == (SEQ, D_MODEL)
    n = batch * seq
    depth = w_in.shape[0]
    x2 = x.reshape(n, d)

    cosb, sinb = _rope_tables(HEAD_DIM)
    cos64, sin64 = _rope_tables(C_ROPE)
    cosc = jnp.concatenate([cos64, cos64], -1)
    sinc = jnp.concatenate([sin64, sin64], -1)

    for l in range(depth):
        w_pad = jnp.pad(w_in[l].astype(BF16), ((0, 0), (0, D_IN_PAD - D_IN)))
        wuq = w_uq[l].astype(BF16).reshape(C_Q_RANK, C_HEADS, C_NOPE + C_ROPE)
        wuq = jnp.concatenate([wuq[..., :C_NOPE].reshape(C_Q_RANK, -1),
                               wuq[..., C_NOPE:].reshape(C_Q_RANK, -1)], -1)
        wukv = w_ukv[l].astype(BF16).reshape(C_KV_RANK, C_HEADS, C_NOPE + C_V)
        wukv = jnp.concatenate([wukv[..., :C_NOPE].reshape(C_KV_RANK, -1),
                                wukv[..., C_NOPE:].reshape(C_KV_RANK, -1)], -1)
        gains = jnp.stack([
            _pad_row(qn_a[l], 512), _pad_row(kn_a[l], 512),
            _pad_row(qn_b[l], 512), _pad_row(kn_b[l], 512),
            _pad_row(cq_norm[l], 512), _pad_row(ckv_norm[l], 512),
            _pad_row(qn_c[l][:C_NOPE], 512),
            _pad_row(jnp.concatenate([qn_c[l][C_NOPE:], qn_c[l][C_NOPE:]]), 512),
            _pad_row(kn_c[l][:C_NOPE], 512), _pad_row(kn_c[l][C_NOPE:], 512),
        ] + [jnp.zeros((512,), F32)] * 6)
        qa, ka, va, qb, kb, vb, qc, kc, vc = _in_proj(
            x2, mix_norm[l].reshape(1, d), w_pad, wuq, wukv, gains, cosb, sinb, cosc, sinc)

        oa = _nbr_attention(qa, ka, va, _nbr_bias_tables(rpb[l]), on_a[l].reshape(1, -1))
        ob = _flash_attention(qb, kb, vb, on_b[l].reshape(1, -1), n_heads=B_HEADS,
                              n_kv_heads=B_KV_HEADS, dq=HEAD_DIM, dv=HEAD_DIM,
                              scale=HEAD_DIM ** -0.5, tq=TQ_FLASH_B, units=2)
        oc = _flash_attention(qc, kc, vc, on_c[l].reshape(1, -1), n_heads=C_HEADS,
                              n_kv_heads=C_HEADS, dq=C_QK_PAD, dv=C_V,
                              scale=(C_NOPE + C_ROPE) ** -0.5, tq=TQ_FLASH_C, units=2)

        wo = w_out[l].astype(BF16)
        wa, wb, wc = wo[:A_WIDTH], wo[A_WIDTH:A_WIDTH + B_WIDTH], wo[A_WIDTH + B_WIDTH:]
        g_ffn = ffn_norm[l].reshape(1, d)
        if l % 2 == 0:
            (x1,) = _out_proj(x2, oa, ob, oc, wa, wb, wc, g_ffn, emit_h=False)
            x2 = _dense_ffn(x1, g_ffn, w_gate[l // 2], w_up[l // 2], w_down[l // 2])
        else:
            x1, h = _out_proj(x2, oa, ob, oc, wa, wb, wc, g_ffn, emit_h=True)
            x2 = _moe_layer(x1, h, router[l // 2], we_gate[l // 2], we_up[l // 2],
                            we_down[l // 2])
    return x2.reshape(batch, seq, d)
```
